```python
import math
import jax, jax.numpy as jnp
from jax import lax
import numpy as np

D_MODEL = 1024
BATCH = 8
SEQ = 4096
DEPTH = 4

A_HEADS = 8
A_KV_HEADS = 2
A_GROUP = A_HEADS // A_KV_HEADS
HEAD_DIM = 64
CMP_LEN = 32
CMP_STRIDE = 16
CMP_HIDDEN = 128
SLC_BLOCK = 64
SLC_TOPK = 8
WINDOW = 512
Q_BLOCK = 128
N_BRANCH = 3
ROPE_THETA = 10000.0
A_WIDTH = A_HEADS * HEAD_DIM
KV_COLS = A_KV_HEADS * HEAD_DIM
B_WIDTH = D_MODEL // 2
B_GROUPS = 8
SHORT_CONV = 3
EVEN_SPLITS = (A_WIDTH, 6 * KV_COLS, A_HEADS * N_BRANCH, 3 * B_WIDTH)
IN_COLS_EVEN = sum(EVEN_SPLITS)
MIX_WIDTH_EVEN = A_WIDTH + B_WIDTH
R_WIDTH = D_MODEL
R_BLOCKS = 8
R_BLOCK_DIM = R_WIDTH // R_BLOCKS
R_CONV = 4
LRU_C = 8.0
D_FF = 2816
FFN_CONV = 3
NORM_EPS = 1e-6
N_EVEN = (DEPTH + 1) // 2
N_ODD = DEPTH // 2

kernel_name = "hybrid_nsa_shortconv_rglru_convffn"


def rmsnorm(x, g):
    xf = x.astype(jnp.float32)
    y = xf * lax.rsqrt(jnp.mean(xf * xf, axis=-1, keepdims=True) + NORM_EPS)
    return (y * g.astype(jnp.float32)).astype(x.dtype)


def causal_dwconv(u, w, b=None):
    k = w.shape[0]
    s = u.shape[1]
    up = jnp.pad(u, ((0, 0), (k - 1, 0), (0, 0)))
    y = sum(w[j] * up[:, j:j + s] for j in range(k))
    if b is not None:
        y = y + b
    return y


def rope_tables(s):
    inv = 1.0 / (ROPE_THETA ** (jnp.arange(0, HEAD_DIM, 2, dtype=jnp.float32) / HEAD_DIM))
    ang = jnp.arange(s, dtype=jnp.float32)[:, None] * inv[None, :]
    ang = jnp.concatenate([ang, ang], axis=-1)
    return jnp.cos(ang), jnp.sin(ang)


def apply_rope(x, cos, sin):
    h = HEAD_DIM // 2
    rot = jnp.concatenate([-x[..., h:], x[..., :h]], axis=-1)
    return (x * cos[:, None, :] + rot * sin[:, None, :]).astype(x.dtype)


def masked_softmax(s, mask):
    s = jnp.where(mask, s.astype(jnp.float32), -1e30)
    m = jnp.max(s, axis=-1, keepdims=True)
    p = jnp.exp(s - m) * mask
    return p / jnp.maximum(jnp.sum(p, axis=-1, keepdims=True), 1e-30)


def compress(kv, pe, w1, w2):
    bsz, s = kv.shape[:2]
    r = CMP_LEN // CMP_STRIDE
    n = s // CMP_STRIDE
    ch = kv.reshape(bsz, n, CMP_STRIDE, A_KV_HEADS, HEAD_DIM)
    blocks = jnp.concatenate([ch[:, j:n - r + 1 + j] for j in range(r)], axis=2)
    blocks = blocks + pe[None, None, :, None, :]
    flat = blocks.transpose(0, 3, 1, 2, 4).reshape(bsz, A_KV_HEADS, n - r + 1, CMP_LEN * HEAD_DIM)
    return jax.nn.gelu(flat @ w1) @ w2


def nsa_attention(q, kc, vc, ks, vs, kw, vw, gates):
    bsz, s = q.shape[:2]
    nc = kc.shape[2]
    nb = s // SLC_BLOCK
    n_sel = min(SLC_TOPK, nb)
    n_qb = s // Q_BLOCK
    scale = HEAD_DIM ** -0.5
    qg = q.reshape(bsz, s, A_KV_HEADS, A_GROUP, HEAD_DIM).transpose(0, 2, 3, 1, 4)
    gg = gates.reshape(bsz, s, A_KV_HEADS, A_GROUP, N_BRANCH).transpose(0, 2, 3, 1, 4)
    ks_blk = ks.transpose(0, 2, 1, 3).reshape(bsz, A_KV_HEADS, nb, SLC_BLOCK, HEAD_DIM)
    vs_blk = vs.transpose(0, 2, 1, 3).reshape(bsz, A_KV_HEADS, nb, SLC_BLOCK, HEAD_DIM)
    pad = ((0, 0), (0, 0), (WINDOW, 0), (0, 0))
    kw_pad = jnp.pad(kw.transpose(0, 2, 1, 3), pad)
    vw_pad = jnp.pad(vw.transpose(0, 2, 1, 3), pad)
    cmp_start = jnp.arange(nc) * CMP_STRIDE
    cmp_end = cmp_start + CMP_LEN - 1
    bidx = jnp.arange(nb)
    blk_start = bidx * SLC_BLOCK
    overlap = ((cmp_start[:, None] <= blk_start[None, :] + SLC_BLOCK - 1)
               & (cmp_end[:, None] >= blk_start[None, :])).astype(jnp.float32)
    gather = jax.vmap(jax.vmap(lambda kb, ix: kb[ix]))

    def block_fn(c):
        t0 = c * Q_BLOCK
        pos = t0 + jnp.arange(Q_BLOCK)
        qb = lax.dynamic_slice_in_dim(qg, t0, Q_BLOCK, axis=3)
        gb = lax.dynamic_slice_in_dim(gg, t0, Q_BLOCK, axis=3)
        sc = jnp.einsum('bhgqd,bhnd->bhgqn', qb, kc) * scale
        p_cmp = masked_softmax(sc, cmp_end[None, :] <= pos[:, None])
        o_cmp = jnp.einsum('bhgqn,bhnd->bhgqd', p_cmp.astype(vc.dtype), vc)
        imp = jnp.einsum('bhqn,nk->bhqk', jnp.sum(p_cmp, axis=2), overlap)
        cur = pos // SLC_BLOCK
        forced = ((bidx[None, :] == 0) | (bidx[None, :] == cur[:, None])
                  | (bidx[None, :] == cur[:, None] - 1))
        valid = bidx[None, :] <= cur[:, None]
        score = jnp.where(forced, 1e4, jnp.where(valid, imp, -1e4))
        _, idx = lax.top_k(score, n_sel)
        ksel = gather(ks_blk, idx).reshape(bsz, A_KV_HEADS, Q_BLOCK, n_sel * SLC_BLOCK, HEAD_DIM)
        vsel = gather(vs_blk, idx).reshape(bsz, A_KV_HEADS, Q_BLOCK, n_sel * SLC_BLOCK, HEAD_DIM)
        kpos = (idx[..., None] * SLC_BLOCK + jnp.arange(SLC_BLOCK)).reshape(
            bsz, A_KV_HEADS, Q_BLOCK, n_sel * SLC_BLOCK)
        ss = jnp.einsum('bhgqd,bhqmd->bhgqm', qb, ksel) * scale
        p_slc = masked_softmax(ss, (kpos <= pos[:, None])[:, :, None])
        o_slc = jnp.einsum('bhgqm,bhqmd->bhgqd', p_slc.astype(vsel.dtype), vsel)
        kwb = lax.dynamic_slice_in_dim(kw_pad, t0, WINDOW + Q_BLOCK, axis=2)
        vwb = lax.dynamic_slice_in_dim(vw_pad, t0, WINDOW + Q_BLOCK, axis=2)
        wpos = t0 - WINDOW + jnp.arange(WINDOW + Q_BLOCK)
        diff = pos[:, None] - wpos[None, :]
        wmask = (diff >= 0) & (diff < WINDOW) & (wpos[None, :] >= 0)
        sw = jnp.einsum('bhgqd,bhkd->bhgqk', qb, kwb) * scale
        p_win = masked_softmax(sw, wmask)
        o_win = jnp.einsum('bhgqk,bhkd->bhgqd', p_win.astype(vwb.dtype), vwb)
        g = jax.nn.sigmoid(gb.astype(jnp.float32))
        o = g[..., 0:1] * o_cmp + g[..., 1:2] * o_slc + g[..., 2:3] * o_win
        return o.astype(q.dtype)

    out = lax.map(block_fn, jnp.arange(n_qb))
    return out.transpose(1, 0, 4, 2, 3, 5).reshape(bsz, s, A_WIDTH)


def even_mixer(h, w_in, cmp_pe, cmp_w1, cmp_w2, conv_w, w_out, cos, sin):
    bsz, s, _ = h.shape
    z = h @ w_in
    o1 = EVEN_SPLITS[0]
    o2 = o1 + EVEN_SPLITS[1]
    o3 = o2 + EVEN_SPLITS[2]
    q = apply_rope(z[..., :o1].reshape(bsz, s, A_HEADS, HEAD_DIM), cos, sin)
    kv = z[..., o1:o2].reshape(bsz, s, 6, A_KV_HEADS, HEAD_DIM)
    gates = z[..., o2:o3]
    bc = z[..., o3:]
    kc = compress(apply_rope(kv[:, :, 0], cos, sin), cmp_pe[0], cmp_w1[0], cmp_w2[0])
    vc = compress(kv[:, :, 1], cmp_pe[1], cmp_w1[1], cmp_w2[1])
    ks = apply_rope(kv[:, :, 2], cos, sin)
    kw = apply_rope(kv[:, :, 4], cos, sin)
    o_a = nsa_attention(q, kc, vc, ks, kv[:, :, 3], kw, kv[:, :, 5], gates)
    bg = bc[..., :B_WIDTH]
    cg = bc[..., B_WIDTH:2 * B_WIDTH]
    xg = bc[..., 2 * B_WIDTH:]
    o_b = bg * causal_dwconv(cg * xg, conv_w)
    return jnp.concatenate([o_a, o_b], axis=-1) @ w_out


def rglru_mixer(h, w_in, b_in, conv_w, conv_b, w_a, b_a, w_i, b_i, lam, w_out, b_out):
    bsz, s, _ = h.shape
    z = h @ w_in + b_in
    y = jax.nn.gelu(z[..., :R_WIDTH])
    u = causal_dwconv(z[..., R_WIDTH:], conv_w, conv_b)
    ub = u.reshape(bsz, s, R_BLOCKS, R_BLOCK_DIM)
    r = jax.nn.sigmoid(jnp.einsum('bsnd,nde->bsne', ub, w_a).reshape(bsz, s, R_WIDTH) + b_a)
    i = jax.nn.sigmoid(jnp.einsum('bsnd,nde->bsne', ub, w_i).reshape(bsz, s, R_WIDTH) + b_i)
    log_a = LRU_C * r.astype(jnp.float32) * jax.nn.log_sigmoid(lam.astype(jnp.float32))
    a = jnp.exp(log_a)
    mult = jnp.sqrt(-jnp.expm1(2.0 * log_a))
    bterm = mult * (i * u).astype(jnp.float32)

    def combine(left, right):
        a1, b1 = left
        a2, b2 = right
        return a2 * a1, a2 * b1 + b2

    _, hs = lax.associative_scan(combine, (a, bterm), axis=1)
    return (hs.astype(h.dtype) * y) @ w_out + b_out


def conv_ffn(h, w_up, conv_w, conv_b, w_down):
    u = causal_dwconv(h @ w_up, conv_w, conv_b)
    return (jax.nn.silu(u[..., :D_FF]) * u[..., D_FF:]) @ w_down


def setup_inputs(seed: int = 0) -> dict:
    key = jax.random.key(seed)
    k = jax.random.split(key, 26)
    f32 = jnp.float32

    def nrm(kk, shape, fan):
        return jax.random.normal(kk, shape, f32) * (fan ** -0.5)

    def small(kk, shape):
        return 0.01 * jax.random.normal(kk, shape, f32)

    a0 = jax.random.uniform(k[17], (N_ODD, R_WIDTH), f32, minval=0.9, maxval=0.999)
    return {
        "x": jax.random.normal(k[0], (BATCH, SEQ, D_MODEL), f32),
        "norm_mix": 1.0 + small(k[1], (DEPTH, D_MODEL)),
        "norm_ffn": 1.0 + small(k[2], (DEPTH, D_MODEL)),
        "norm_final": 1.0 + small(k[3], (D_MODEL,)),
        "a_w_in": nrm(k[4], (N_EVEN, D_MODEL, IN_COLS_EVEN), D_MODEL),
        "a_cmp_pe": 0.1 * jax.random.normal(k[5], (N_EVEN, 2, CMP_LEN, HEAD_DIM), f32),
        "a_cmp_w1": nrm(k[6], (N_EVEN, 2, CMP_LEN * HEAD_DIM, CMP_HIDDEN), CMP_LEN * HEAD_DIM),
        "a_cmp_w2": nrm(k[7], (N_EVEN, 2, CMP_HIDDEN, HEAD_DIM), CMP_HIDDEN),
        "a_conv_w": nrm(k[8], (N_EVEN, SHORT_CONV, B_WIDTH), SHORT_CONV),
        "a_w_out": nrm(k[9], (N_EVEN, MIX_WIDTH_EVEN, D_MODEL), MIX_WIDTH_EVEN),
        "c_w_in": nrm(k[10], (N_ODD, D_MODEL, 2 * R_WIDTH), D_MODEL),
        "c_b_in": small(k[11], (N_ODD, 2 * R_WIDTH)),
        "c_conv_w": nrm(k[12], (N_ODD, R_CONV, R_WIDTH), R_CONV),
        "c_conv_b": small(k[13], (N_ODD, R_WIDTH)),
        "c_w_a": nrm(k[14], (N_ODD, R_BLOCKS, R_BLOCK_DIM, R_BLOCK_DIM), R_BLOCK_DIM),
        "c_b_a": small(k[15], (N_ODD, R_WIDTH)),
        "c_w_i": nrm(k[16], (N_ODD, R_BLOCKS, R_BLOCK_DIM, R_BLOCK_DIM), R_BLOCK_DIM),
        "c_b_i": small(k[18], (N_ODD, R_WIDTH)),
        "c_lambda": jnp.log(a0) - jnp.log1p(-a0),
        "c_w_out": nrm(k[19], (N_ODD, R_WIDTH, D_MODEL), R_WIDTH),
        "c_b_out": small(k[20], (N_ODD, D_MODEL)),
        "f_w_up": nrm(k[21], (DEPTH, D_MODEL, 2 * D_FF), D_MODEL),
        "f_conv_w": nrm(k[22], (DEPTH, FFN_CONV, 2 * D_FF), FFN_CONV),
        "f_conv_b": small(k[23], (DEPTH, 2 * D_FF)),
        "f_w_down": nrm(k[24], (DEPTH, D_FF, D_MODEL), D_FF),
    }


def reference(x, norm_mix, norm_ffn, norm_final, a_w_in, a_cmp_pe, a_cmp_w1, a_cmp_w2,
              a_conv_w, a_w_out, c_w_in, c_b_in, c_conv_w, c_conv_b, c_w_a, c_b_a,
              c_w_i, c_b_i, c_lambda, c_w_out, c_b_out, f_w_up, f_conv_w, f_conv_b,
              f_w_down):
    cos, sin = rope_tables(x.shape[1])
    for layer in range(DEPTH):
        h = rmsnorm(x, norm_mix[layer])
        j = layer // 2
        if layer % 2 == 0:
            mix = even_mixer(h, a_w_in[j], a_cmp_pe[j], a_cmp_w1[j], a_cmp_w2[j],
                             a_conv_w[j], a_w_out[j], cos, sin)
        else:
            mix = rglru_mixer(h, c_w_in[j], c_b_in[j], c_conv_w[j], c_conv_b[j],
                              c_w_a[j], c_b_a[j], c_w_i[j], c_b_i[j], c_lambda[j],
                              c_w_out[j], c_b_out[j])
        x = x + mix
        x = x + conv_ffn(rmsnorm(x, norm_ffn[layer]), f_w_up[layer], f_conv_w[layer],
                         f_conv_b[layer], f_w_down[layer])
    return rmsnorm(x, norm_final)
```

```python
import functools
import math

import numpy as np
import jax
import jax.numpy as jnp
from jax import lax
from jax.experimental import pallas as pl
from jax.experimental.pallas import tpu as pltpu

F32 = jnp.float32
BF16 = jnp.bfloat16

D_MODEL = 1024
DEPTH = 4
A_HEADS = 8
A_KV_HEADS = 2
A_GROUP = A_HEADS // A_KV_HEADS
HEAD_DIM = 64
CMP_LEN = 32
CMP_STRIDE = 16
CMP_HIDDEN = 128
SLC_BLOCK = 64
SLC_TOPK = 8
WINDOW = 512
Q_BLOCK = 128
N_BRANCH = 3
ROPE_THETA = 10000.0
A_WIDTH = A_HEADS * HEAD_DIM
KV_COLS = A_KV_HEADS * HEAD_DIM
B_WIDTH = D_MODEL // 2
R_WIDTH = D_MODEL
R_BLOCKS = 8
R_BLOCK_DIM = R_WIDTH // R_BLOCKS
R_CONV = 4
LRU_C = 8.0
D_FF = 2816
NORM_EPS = 1e-6

LANE = 128
SUBLANE = 8
V7X_VMEM_BYTES = 64 * 1024 * 1024
VMEM_LIMIT = (V7X_VMEM_BYTES * 7) // 8

ROW_TILE = 512
KEY_CHUNK = 256
FF_CHUNK = 512
M_FLOOR = -1e30
LOG2E = 1.4426950408889634

_L = np.arange(LANE)
_IL_HEAD = (_L // 32) % 2
_IL_DIM = (_L % 32) + 32 * (_L // 64)
_ST_HEAD = _L // HEAD_DIM
_ST_DIM = _L % HEAD_DIM
_K_PERM = _IL_HEAD * HEAD_DIM + _IL_DIM
_Q_PERM = np.concatenate([(_IL_HEAD * A_GROUP + g) * HEAD_DIM + _IL_DIM for g in range(A_GROUP)])
_G_PERM = np.array([hk * A_GROUP * N_BRANCH + g * N_BRANCH + br
                    for br in range(N_BRANCH) for hk in range(A_KV_HEADS) for g in range(A_GROUP)])

_C_Q = 0
_C_KCS = 512
_C_KS = 640
_C_KW = 768
_C_VCS = 896
_C_VS = 1024
_C_VW = 1152
_C_GATE = 1280
_C_BG = 1408
_C_CG = 1920
_C_XG = 2432
_C_END = 2944


def _resident(shape):
    nd = len(shape)
    return pl.BlockSpec(shape, lambda *_: (0,) * nd, pipeline_mode=pl.Buffered(1))


def _params(n_axes):
    return pltpu.CompilerParams(dimension_semantics=("arbitrary",) * n_axes,
                                vmem_limit_bytes=VMEM_LIMIT)


def _rms(x, g):
    return (x * lax.rsqrt(jnp.mean(x * x, axis=-1, keepdims=True) + NORM_EPS)) * g


def _gelu(x):
    return 0.5 * x * (1.0 + jnp.tanh(0.7978845608028654 * (x + 0.044715 * (x * x * x))))


def _dot(a, b):
    return jnp.dot(a, b, preferred_element_type=F32)


def _dot_nt(a, b):
    return lax.dot_general(a, b, (((1,), (1,)), ((), ())), preferred_element_type=F32)


def _dot_tn(a, b):
    return lax.dot_general(a, b, (((0,), (0,)), ((), ())), preferred_element_type=F32)


def _even_in_kernel(x_ref, g_ref, w_ref, cos_ref, sin_ref, cw_ref,
                    q_ref, kcs_ref, ks_ref, kw_ref, vcs_ref, vst_ref, vwt_ref, gate_ref, ob_ref,
                    cbuf, *, tiles_per_batch):
    tm = x_ref.shape[0]
    i = pl.program_id(0)
    hb = _rms(x_ref[...], g_ref[...]).astype(BF16)
    cos = cos_ref[...]
    sin = sin_ref[...]

    def rope(z):
        return z * cos + pltpu.roll(z, LANE // 2, 1) * sin

    zq = _dot(hb, w_ref[:, _C_Q:_C_KCS])
    for g in range(A_GROUP):
        q_ref[:, g * LANE:(g + 1) * LANE] = rope(zq[:, g * LANE:(g + 1) * LANE]).astype(BF16)

    zk = _dot(hb, w_ref[:, _C_KCS:_C_VS])
    kcs_ref[...] = rope(zk[:, 0:LANE]).astype(BF16)
    ks_ref[...] = rope(zk[:, LANE:2 * LANE]).astype(BF16)
    kw_ref[...] = rope(zk[:, 2 * LANE:3 * LANE]).astype(BF16)
    vcs_ref[...] = zk[:, 3 * LANE:4 * LANE].astype(BF16)

    zv = _dot(hb, w_ref[:, _C_VS:_C_BG])
    for c in range(tm // KEY_CHUNK):
        rows = slice(c * KEY_CHUNK, (c + 1) * KEY_CHUNK)
        vst_ref[0, c] = zv[rows, 0:LANE].T.astype(BF16)
        vwt_ref[0, c] = zv[rows, LANE:2 * LANE].T.astype(BF16)
    gate_ref[...] = jax.nn.sigmoid(zv[:, 2 * LANE:3 * LANE])

    @pl.when(i % tiles_per_batch == 0)
    def _():
        cbuf[0:SUBLANE, :] = jnp.zeros((SUBLANE, B_WIDTH), F32)

    c0 = _dot(hb, w_ref[:, _C_CG:_C_XG]) * _dot(hb, w_ref[:, _C_XG:_C_END])
    cbuf[SUBLANE:SUBLANE + tm, :] = c0
    cw = cw_ref[...]
    y = (cw[0:1, :] * cbuf[pl.ds(SUBLANE - 2, tm), :]
         + cw[1:2, :] * cbuf[pl.ds(SUBLANE - 1, tm), :]
         + cw[2:3, :] * c0)
    cbuf[0:SUBLANE, :] = c0[tm - SUBLANE:tm, :]
    ob_ref[...] = (_dot(hb, w_ref[:, _C_BG:_C_CG]) * y).astype(BF16)


def _even_in_proj(x2, g, wcat, cos, sin, conv_w, batch, seq):
    n = x2.shape[0]
    tm = ROW_TILE
    tpb = seq // tm
    nch = seq // KEY_CHUNK
    row = lambda w: pl.BlockSpec((tm, w), lambda i: (i, 0))
    vt_spec = pl.BlockSpec((1, tm // KEY_CHUNK, LANE, KEY_CHUNK), lambda i: (i // tpb, i % tpb, 0, 0))
    tab_spec = pl.BlockSpec((tm, LANE), lambda i: (i % tpb, 0))
    out_shape = (
        jax.ShapeDtypeStruct((n, A_WIDTH), BF16),
        jax.ShapeDtypeStruct((n, LANE), BF16),
        jax.ShapeDtypeStruct((n, LANE), BF16),
        jax.ShapeDtypeStruct((n, LANE), BF16),
        jax.ShapeDtypeStruct((n, LANE), BF16),
        jax.ShapeDtypeStruct((batch, nch, LANE, KEY_CHUNK), BF16),
        jax.ShapeDtypeStruct((batch, nch, LANE, KEY_CHUNK), BF16),
        jax.ShapeDtypeStruct((n, LANE), F32),
        jax.ShapeDtypeStruct((n, B_WIDTH), BF16),
    )
    return pl.pallas_call(
        functools.partial(_even_in_kernel, tiles_per_batch=tpb),
        out_shape=out_shape,
        grid=(n // tm,),
        in_specs=[row(D_MODEL), _resident((1, D_MODEL)), _resident((D_MODEL, _C_END)),
                  tab_spec, tab_spec, _resident((3, B_WIDTH))],
        out_specs=(row(A_WIDTH), row(LANE), row(LANE), row(LANE), row(LANE),
                   vt_spec, vt_spec, row(LANE), row(B_WIDTH)),
        scratch_shapes=[pltpu.VMEM((tm + SUBLANE, B_WIDTH), F32)],
        compiler_params=_params(1),
        name="even_in_proj",
    )(x2, g, wcat, cos, sin, conv_w)


def _compress_kernel(k2_ref, v2_ref, w1k_ref, w2k_ref, pek_ref, w1v_ref, w2v_ref, pev_ref,
                     kc_ref, vct_ref, sbuf):
    ncp = k2_ref.shape[1]

    def one(x2, w1_ref, w2_ref, pe_ref):
        top = _dot(x2, w1_ref[0])
        sbuf[0:ncp, :] = _dot(x2, w1_ref[1])
        sbuf[ncp:ncp + SUBLANE, :] = jnp.zeros((SUBLANE, sbuf.shape[1]), F32)
        bias = _dot(pe_ref[0], w1_ref[0]) + _dot(pe_ref[1], w1_ref[1])
        hid = top + sbuf[pl.ds(1, ncp), :] + bias[0:1, :]
        return _dot(_gelu(hid).astype(BF16), w2_ref[...])

    kc_ref[0] = one(k2_ref[0], w1k_ref, w2k_ref, pek_ref).astype(BF16)
    vct_ref[0] = one(v2_ref[0], w1v_ref, w2v_ref, pev_ref).T.astype(BF16)


def _compress(k2, v2, w1k, w2k, pek, w1v, w2v, pev):
    batch, ncp, width = k2.shape
    hid2 = A_KV_HEADS * CMP_HIDDEN
    src = pl.BlockSpec((1, ncp, width), lambda b: (b, 0, 0))
    return pl.pallas_call(
        _compress_kernel,
        out_shape=(jax.ShapeDtypeStruct((batch, ncp, LANE), BF16),
                   jax.ShapeDtypeStruct((batch, LANE, ncp), BF16)),
        grid=(batch,),
        in_specs=[src, src,
                  _resident((2, width, hid2)), _resident((hid2, LANE)), _resident((2, SUBLANE, width)),
                  _resident((2, width, hid2)), _resident((hid2, LANE)), _resident((2, SUBLANE, width))],
        out_specs=(pl.BlockSpec((1, ncp, LANE), lambda b: (b, 0, 0)),
                   pl.BlockSpec((1, LANE, ncp), lambda b: (b, 0, 0))),
        scratch_shapes=[pltpu.VMEM((ncp + SUBLANE, hid2), F32)],
        compiler_params=_params(1),
        name="compress",
    )(k2, v2, w1k, w2k, pek, w1v, w2v, pev)


def _attn_kernel(q_ref, kc_ref, vct_ref, ks_ref, vst_ref, kw_ref, vwt_ref, gate_ref, o_ref,
                 ps_ref, selx_ref, gt_ref, *, n_sel):
    hk = pl.program_id(1)
    qb = pl.program_id(2)
    ncp = kc_ref.shape[1]
    nb = ncp // (SLC_BLOCK // CMP_STRIDE)
    nq = A_GROUP * Q_BLOCK
    t0 = qb * Q_BLOCK

    lane = lax.broadcasted_iota(jnp.int32, (Q_BLOCK, LANE), 1)
    head_mask = jnp.where((lane // 32) % 2 == hk, 1.0, 0.0).astype(BF16)
    qmat = jnp.concatenate([q_ref[:, g * LANE:(g + 1) * LANE] * head_mask for g in range(A_GROUP)],
                           axis=0)
    pos = t0 + lax.broadcasted_iota(jnp.int32, (1, nq), 1) % Q_BLOCK

    sc = _dot_nt(kc_ref[0], qmat)
    cend = lax.broadcasted_iota(jnp.int32, (ncp, 1), 0) * CMP_STRIDE + (CMP_LEN - 1)
    sc = jnp.where(cend <= pos, sc, -jnp.inf)
    m_c = jnp.maximum(jnp.max(sc, axis=0, keepdims=True), M_FLOOR)
    p_c = jnp.exp2(sc - m_c)
    l_c = jnp.sum(p_c, axis=0, keepdims=True)
    pn = p_c * (1.0 / jnp.maximum(l_c, 1e-30))
    o_cmp = _dot(vct_ref[0], pn.astype(BF16))

    psum = pn[:, 0:Q_BLOCK]
    for g in range(1, A_GROUP):
        psum = psum + pn[:, g * Q_BLOCK:(g + 1) * Q_BLOCK]
    ps_ref[0:SUBLANE, :] = jnp.zeros((SUBLANE, Q_BLOCK), F32)
    ps_ref[SUBLANE:SUBLANE + ncp, :] = psum
    per = SLC_BLOCK // CMP_STRIDE
    imp = ps_ref[pl.ds(SUBLANE - 1, nb, stride=per), :]
    for k in range(per):
        imp = imp + ps_ref[pl.ds(SUBLANE + k, nb, stride=per), :]
    bidx = lax.broadcasted_iota(jnp.int32, (nb, Q_BLOCK), 0)
    cur = (t0 + lax.broadcasted_iota(jnp.int32, (1, Q_BLOCK), 1)) // SLC_BLOCK
    forced = (bidx == 0) | (bidx == cur) | (bidx == cur - 1)
    score = jnp.where(forced, 1e4, jnp.where(bidx <= cur, imp, -1e4))
    rank = jnp.zeros((nb, Q_BLOCK), F32)
    for k in range(nb):
        rk = score[k:k + 1, :]
        beats = (rk > score) | ((rk == score) & (bidx > k))
        rank = rank + jnp.where(beats, 1.0, 0.0)
    sel = jnp.where(rank < n_sel, 1.0, 0.0)
    for b in range(nb):
        selx_ref[b] = jnp.broadcast_to(sel[b:b + 1, :], (SUBLANE, Q_BLOCK))

    def chunk(k_ref, vt_ref, c, carry, mask_fn):
        m, l, acc = carry
        k0 = pl.multiple_of(c * KEY_CHUNK, KEY_CHUNK)
        s = mask_fn(_dot_nt(k_ref[0, pl.ds(k0, KEY_CHUNK), :], qmat), c, k0)
        m_new = jnp.maximum(m, jnp.max(s, axis=0, keepdims=True))
        alpha = jnp.exp2(m - m_new)
        p = jnp.exp2(s - m_new)
        l = alpha * l + jnp.sum(p, axis=0, keepdims=True)
        acc = alpha * acc + _dot(vt_ref[0, c], p.astype(BF16))
        return m_new, l, acc

    init = (jnp.full((1, nq), M_FLOOR, F32), jnp.zeros((1, nq), F32), jnp.zeros((HEAD_DIM, nq), F32))
    blk_rows = lax.broadcasted_iota(jnp.int32, (SLC_BLOCK, 1), 0)

    def slc_mask(causal):
        def fn(s, c, k0):
            slabs = []
            for bl in range(KEY_CHUNK // SLC_BLOCK):
                r = selx_ref[c * (KEY_CHUNK // SLC_BLOCK) + bl]
                keep = jnp.concatenate([jnp.concatenate([r] * A_GROUP, axis=1)] * (SLC_BLOCK // SUBLANE),
                                       axis=0) > 0.5
                if causal:
                    keep = keep & (k0 + bl * SLC_BLOCK + blk_rows <= pos)
                slabs.append(jnp.where(keep, s[bl * SLC_BLOCK:(bl + 1) * SLC_BLOCK, :], -jnp.inf))
            return jnp.concatenate(slabs, axis=0)
        return fn

    last = qb // (KEY_CHUNK // Q_BLOCK)
    carry = lax.fori_loop(0, last, lambda c, cr: chunk(ks_ref, vst_ref, c, cr, slc_mask(False)), init)
    _, l_s, acc_s = chunk(ks_ref, vst_ref, last, carry, slc_mask(True))

    chunk_rows = lax.broadcasted_iota(jnp.int32, (KEY_CHUNK, 1), 0)

    def win_mask(s, c, k0):
        diff = pos - (k0 + chunk_rows)
        return jnp.where((diff >= 0) & (diff < WINDOW), s, -jnp.inf)

    first = jnp.maximum(last - WINDOW // KEY_CHUNK, 0)
    _, l_w, acc_w = lax.fori_loop(first, last + 1,
                                  lambda c, cr: chunk(kw_ref, vwt_ref, c, cr, win_mask), init)

    gt_ref[...] = gate_ref[...].T

    def gate(br):
        base = br * A_HEADS + hk * A_GROUP
        return jnp.concatenate([gt_ref[pl.ds(base + g, 1), :] for g in range(A_GROUP)], axis=1)

    o = (gate(0) * o_cmp
         + gate(1) * (acc_s * (1.0 / jnp.maximum(l_s, 1e-30)))
         + gate(2) * (acc_w * (1.0 / jnp.maximum(l_w, 1e-30))))
    for g in range(A_GROUP):
        o_ref[0, g * HEAD_DIM:(g + 1) * HEAD_DIM, :] = o[:, g * Q_BLOCK:(g + 1) * Q_BLOCK].astype(BF16)


def _attention(q, kc, vct, ks, vst, kw, vwt, gates, batch, seq):
    nqb = seq // Q_BLOCK
    ncp = kc.shape[1]
    nch = seq // KEY_CHUNK
    nb = seq // SLC_BLOCK
    ks3 = ks.reshape(batch, seq, LANE)
    kw3 = kw.reshape(batch, seq, LANE)
    kspec = pl.BlockSpec((1, seq, LANE), lambda b, h, t: (b, 0, 0))
    vspec = pl.BlockSpec((1, nch, HEAD_DIM, KEY_CHUNK), lambda b, h, t: (b, 0, h, 0))
    return pl.pallas_call(
        functools.partial(_attn_kernel, n_sel=min(SLC_TOPK, nb)),
        out_shape=jax.ShapeDtypeStruct((batch, A_WIDTH, seq), BF16),
        grid=(batch, A_KV_HEADS, nqb),
        in_specs=[pl.BlockSpec((Q_BLOCK, A_WIDTH), lambda b, h, t: (b * nqb + t, 0)),
                  pl.BlockSpec((1, ncp, LANE), lambda b, h, t: (b, 0, 0)),
                  pl.BlockSpec((1, HEAD_DIM, ncp), lambda b, h, t: (b, h, 0)),
                  kspec, vspec, kspec, vspec,
                  pl.BlockSpec((Q_BLOCK, LANE), lambda b, h, t: (b * nqb + t, 0))],
        out_specs=pl.BlockSpec((1, A_GROUP * HEAD_DIM, Q_BLOCK), lambda b, h, t: (b, h, t)),
        scratch_shapes=[pltpu.VMEM((ncp + SUBLANE, Q_BLOCK), F32),
                        pltpu.VMEM((nb, SUBLANE, Q_BLOCK), F32),
                        pltpu.VMEM((LANE, Q_BLOCK), F32)],
        compiler_params=_params(3),
        name="nsa_attention",
    )(q, kc, vct, ks3, vst, kw3, vwt, gates)


def _out_proj_kernel(x_ref, ot_ref, ob_ref, wa_ref, wb_ref, o_ref):
    o_ref[...] = x_ref[...] + _dot_tn(ot_ref[0], wa_ref[...]) + _dot(ob_ref[...], wb_ref[...])


def _out_proj(x2, o_t, o_b, wa, wb, seq):
    n = x2.shape[0]
    tm = ROW_TILE
    tpb = seq // tm
    row = lambda w: pl.BlockSpec((tm, w), lambda i: (i, 0))
    return pl.pallas_call(
        _out_proj_kernel,
        out_shape=jax.ShapeDtypeStruct((n, D_MODEL), F32),
        grid=(n // tm,),
        in_specs=[row(D_MODEL), pl.BlockSpec((1, A_WIDTH, tm), lambda i: (i // tpb, 0, i % tpb)),
                  row(B_WIDTH), _resident((A_WIDTH, D_MODEL)), _resident((B_WIDTH, D_MODEL))],
        out_specs=row(D_MODEL),
        compiler_params=_params(1),
        name="even_out_proj",
    )(x2, o_t, o_b, wa, wb)


def _rglru_kernel(x_ref, g_ref, win_ref, bin_ref, cw_ref, cb_ref, wai_ref, ba_ref, bi_ref, lam_ref,
                  wout_ref, bout_ref, o_ref, zbuf, abuf, bbuf, hbuf, hprev, *, tiles_per_batch):
    tm = x_ref.shape[0]
    i = pl.program_id(0)

    @pl.when(i % tiles_per_batch == 0)
    def _():
        zbuf[0:SUBLANE, :] = jnp.zeros((SUBLANE, R_WIDTH), F32)
        hprev[...] = jnp.zeros((SUBLANE, R_WIDTH), F32)

    x = x_ref[...]
    hb = _rms(x, g_ref[...]).astype(BF16)
    y = _gelu(_dot(hb, win_ref[:, 0:R_WIDTH]) + bin_ref[:, 0:R_WIDTH])
    z2 = _dot(hb, win_ref[:, R_WIDTH:2 * R_WIDTH]) + bin_ref[:, R_WIDTH:2 * R_WIDTH]
    zbuf[SUBLANE:SUBLANE + tm, :] = z2
    cw = cw_ref[...]
    u = (cw[0:1, :] * zbuf[pl.ds(SUBLANE - 3, tm), :]
         + cw[1:2, :] * zbuf[pl.ds(SUBLANE - 2, tm), :]
         + cw[2:3, :] * zbuf[pl.ds(SUBLANE - 1, tm), :]
         + cw[3:4, :] * z2 + cb_ref[...])
    zbuf[0:SUBLANE, :] = z2[tm - SUBLANE:tm, :]

    ub = u.astype(BF16)
    log_sig = jax.nn.log_sigmoid(lam_ref[...])
    for n in range(R_BLOCKS):
        cols = slice(n * R_BLOCK_DIM, (n + 1) * R_BLOCK_DIM)
        ri = _dot(ub[:, cols], wai_ref[n])
        r = jax.nn.sigmoid(ri[:, 0:R_BLOCK_DIM] + ba_ref[:, cols])
        gi = jax.nn.sigmoid(ri[:, R_BLOCK_DIM:2 * R_BLOCK_DIM] + bi_ref[:, cols])
        log_a = (LRU_C * log_sig[:, cols]) * r
        a = jnp.exp(log_a)
        mult = jnp.sqrt(-jnp.tanh(log_a) * (a * a + 1.0))
        abuf[:, cols] = a
        bbuf[:, cols] = mult * (gi * u[:, cols])

    rows = lax.broadcasted_iota(jnp.int32, (SUBLANE, R_WIDTH), 0)

    def slab(j, hp):
        r0 = pl.multiple_of(j * SUBLANE, SUBLANE)
        a = abuf[pl.ds(r0, SUBLANE), :]
        b = bbuf[pl.ds(r0, SUBLANE), :]
        for d in (1, 2, 4):
            keep = rows >= d
            b = jnp.where(keep, a * pltpu.roll(b, d, 0) + b, b)
            a = jnp.where(keep, a * pltpu.roll(a, d, 0), a)
        h = a * hp + b
        hbuf[pl.ds(r0, SUBLANE), :] = h
        return jnp.broadcast_to(h[SUBLANE - 1:SUBLANE, :], (SUBLANE, R_WIDTH))

    hprev[...] = lax.fori_loop(0, tm // SUBLANE, slab, hprev[...])
    o_ref[...] = x + _dot((hbuf[...] * y).astype(BF16), wout_ref[...]) + bout_ref[...]


def _rglru(x2, g, win, b_in, conv_w, conv_b, wai, b_a, b_i, lam, wout, b_out, seq):
    n = x2.shape[0]
    tm = ROW_TILE
    tpb = seq // tm
    row = pl.BlockSpec((tm, D_MODEL), lambda i: (i, 0))
    buf = lambda r: pltpu.VMEM((r, R_WIDTH), F32)
    return pl.pallas_call(
        functools.partial(_rglru_kernel, tiles_per_batch=tpb),
        out_shape=jax.ShapeDtypeStruct((n, D_MODEL), F32),
        grid=(n // tm,),
        in_specs=[row, _resident((1, D_MODEL)), _resident((D_MODEL, 2 * R_WIDTH)), _resident((1, 2 * R_WIDTH)),
                  _resident((R_CONV, R_WIDTH)), _resident((1, R_WIDTH)),
                  _resident((R_BLOCKS, R_BLOCK_DIM, 2 * R_BLOCK_DIM)),
                  _resident((1, R_WIDTH)), _resident((1, R_WIDTH)), _resident((1, R_WIDTH)),
                  _resident((R_WIDTH, D_MODEL)), _resident((1, D_MODEL))],
        out_specs=row,
        scratch_shapes=[buf(tm + SUBLANE), buf(tm), buf(tm), buf(tm), buf(SUBLANE)],
        compiler_params=_params(1),
        name="rglru_mixer",
    )(x2, g, win, b_in, conv_w, conv_b, wai, b_a, b_i, lam, wout, b_out)


def _ffn_kernel(x_ref, g_ref, wup_ref, cw_ref, cb_ref, wdn_ref, gf_ref, o_ref, ubuf, carry, acc,
                *, tiles_per_batch, final_norm):
    tm = x_ref.shape[0]
    i = pl.program_id(0)

    @pl.when(i % tiles_per_batch == 0)
    def _():
        carry[...] = jnp.zeros(carry.shape, F32)

    x = x_ref[...]
    hb = _rms(x, g_ref[...]).astype(BF16)
    acc[...] = x
    for c0 in range(0, D_FF, FF_CHUNK):
        width = min(FF_CHUNK, D_FF - c0)
        branch = []
        for half in range(2):
            col = half * D_FF + c0
            cols = slice(col, col + width)
            u = _dot(hb, wup_ref[:, cols])
            ubuf[half, 0:SUBLANE, 0:width] = carry[:, cols]
            ubuf[half, SUBLANE:SUBLANE + tm, 0:width] = u
            carry[:, cols] = u[tm - SUBLANE:tm, :]
            w = cw_ref[:, cols]
            branch.append(w[0:1, :] * ubuf[half, pl.ds(SUBLANE - 2, tm), 0:width]
                          + w[1:2, :] * ubuf[half, pl.ds(SUBLANE - 1, tm), 0:width]
                          + w[2:3, :] * u + cb_ref[:, cols])
        act = (branch[0] * jax.nn.sigmoid(branch[0]) * branch[1]).astype(BF16)
        acc[...] += _dot(act, wdn_ref[c0:c0 + width, :])
    out = acc[...]
    if final_norm:
        out = _rms(out, gf_ref[...])
    o_ref[...] = out


def _ffn(x2, g, wup, conv_w, conv_b, wdn, g_final, seq, final_norm):
    n = x2.shape[0]
    tm = ROW_TILE
    tpb = seq // tm
    row = pl.BlockSpec((tm, D_MODEL), lambda i: (i, 0))
    return pl.pallas_call(
        functools.partial(_ffn_kernel, tiles_per_batch=tpb, final_norm=final_norm),
        out_shape=jax.ShapeDtypeStruct((n, D_MODEL), F32),
        grid=(n // tm,),
        in_specs=[row, _resident((1, D_MODEL)), _resident((D_MODEL, 2 * D_FF)), _resident((3, 2 * D_FF)),
                  _resident((1, 2 * D_FF)), _resident((D_FF, D_MODEL)), _resident((1, D_MODEL))],
        out_specs=row,
        scratch_shapes=[pltpu.VMEM((2, tm + SUBLANE, FF_CHUNK), F32),
                        pltpu.VMEM((SUBLANE, 2 * D_FF), F32),
                        pltpu.VMEM((tm, D_MODEL), F32)],
        compiler_params=_params(1),
        name="conv_ffn",
    )(x2, g, wup, conv_w, conv_b, wdn, g_final)


def _rope_tables(seq):
    inv = 1.0 / (ROPE_THETA ** (jnp.arange(0, HEAD_DIM, 2, dtype=F32) / HEAD_DIM))
    ang = jnp.arange(seq, dtype=F32)[:, None] * inv[None, :]
    ang = jnp.tile(ang, (1, LANE // (HEAD_DIM // 2)))
    sign = jnp.where(jnp.arange(LANE) < LANE // 2, -1.0, 1.0).astype(F32)
    return jnp.cos(ang), jnp.sin(ang) * sign[None, :]


def _even_in_weights(w_in):
    kv0 = A_WIDTH
    part = lambda p: w_in[:, kv0 + p * KV_COLS: kv0 + (p + 1) * KV_COLS]
    g0 = kv0 + 6 * KV_COLS
    bc0 = g0 + A_HEADS * N_BRANCH
    scale = (HEAD_DIM ** -0.5) * LOG2E
    gates = jnp.pad(w_in[:, g0:bc0][:, _G_PERM], ((0, 0), (0, LANE - A_HEADS * N_BRANCH)))
    return jnp.concatenate([
        w_in[:, :A_WIDTH][:, _Q_PERM] * scale,
        part(0)[:, _K_PERM], part(2)[:, _K_PERM], part(4)[:, _K_PERM],
        part(1), part(3), part(5),
        gates,
        w_in[:, bc0:],
    ], axis=1).astype(BF16)


def _compress_weights(pe, w1, w2, lane_head, lane_dim):
    onehot = jnp.asarray(lane_head[:, None] == np.arange(A_KV_HEADS)[None, :], F32)
    w1g = w1.reshape(CMP_LEN, HEAD_DIM, CMP_HIDDEN)[:, lane_dim, :]
    w1x = (w1g[:, :, None, :] * onehot[None, :, :, None]).reshape(CMP_LEN * LANE, A_KV_HEADS * CMP_HIDDEN)
    half = (CMP_LEN // 2) * LANE
    w1x = jnp.stack([w1x[:half], w1x[half:]]).astype(BF16)
    w2x = (onehot.T[:, None, :] * w2[:, lane_dim][None, :, :]).reshape(A_KV_HEADS * CMP_HIDDEN, LANE)
    pel = pe[:, lane_dim].reshape(2, 1, half)
    pex = jnp.broadcast_to(pel, (2, SUBLANE, half)).astype(BF16)
    return w1x, w2x.astype(BF16), pex


def kernel(x, norm_mix, norm_ffn, norm_final, a_w_in, a_cmp_pe, a_cmp_w1, a_cmp_w2, a_conv_w, a_w_out,
           c_w_in, c_b_in, c_conv_w, c_conv_b, c_w_a, c_b_a, c_w_i, c_b_i, c_lambda, c_w_out, c_b_out,
           f_w_up, f_conv_w, f_conv_b, f_w_down):
    batch, seq, _ = x.shape
    assert seq % ROW_TILE == 0 and ROW_TILE % KEY_CHUNK == 0 and seq // SLC_BLOCK >= 1
    x2 = x.reshape(batch * seq, D_MODEL)
    cos, sin = _rope_tables(seq)
    row = lambda v: v.reshape(1, -1)
    for layer in range(DEPTH):
        j = layer // 2
        g_mix = row(norm_mix[layer])
        if layer % 2 == 0:
            wcat = _even_in_weights(a_w_in[j])
            q, kcs, ks, kw, vcs, vst, vwt, gates, o_b = _even_in_proj(
                x2, g_mix, wcat, cos, sin, a_conv_w[j], batch, seq)
            ncp = seq // CMP_STRIDE
            w1k, w2k, pek = _compress_weights(a_cmp_pe[j, 0], a_cmp_w1[j, 0], a_cmp_w2[j, 0], _IL_HEAD, _IL_DIM)
            w1v, w2v, pev = _compress_weights(a_cmp_pe[j, 1], a_cmp_w1[j, 1], a_cmp_w2[j, 1], _ST_HEAD, _ST_DIM)
            kc, vct = _compress(kcs.reshape(batch, ncp, CMP_STRIDE * LANE),
                                vcs.reshape(batch, ncp, CMP_STRIDE * LANE),
                                w1k, w2k, pek, w1v, w2v, pev)
            o_t = _attention(q, kc, vct, ks, vst, kw, vwt, gates, batch, seq)
            w_out = a_w_out[j].astype(BF16)
            x2 = _out_proj(x2, o_t, o_b, w_out[:A_WIDTH], w_out[A_WIDTH:], seq)
        else:
            wai = jnp.concatenate([c_w_a[j], c_w_i[j]], axis=-1).astype(BF16)
            x2 = _rglru(x2, g_mix, c_w_in[j].astype(BF16), row(c_b_in[j]), c_conv_w[j], row(c_conv_b[j]),
                        wai, row(c_b_a[j]), row(c_b_i[j]), row(c_lambda[j]),
                        c_w_out[j].astype(BF16), row(c_b_out[j]), seq)
        x2 = _ffn(x2, row(norm_ffn[layer]), f_w_up[layer].astype(BF16), f_conv_w[layer],
                  row(f_conv_b[layer]), f_w_down[layer].astype(BF16), row(norm_final), seq,
                  final_norm=(layer == DEPTH - 1))
    return x2.reshape(batch, seq, D_MODEL)
```

```python
import functools
import math

import numpy as np
import jax
import jax.numpy as jnp
from jax import lax
from jax.experimental import pallas as pl
from jax.experimental.pallas import tpu as pltpu

F32 = jnp.float32
BF16 = jnp.bfloat16

D_MODEL = 1024
DEPTH = 4
A_HEADS = 8
A_KV_HEADS = 2
A_GROUP = A_HEADS // A_KV_HEADS
HEAD_DIM = 64
CMP_LEN = 32
CMP_STRIDE = 16
CMP_HIDDEN = 128
SLC_BLOCK = 64
SLC_TOPK = 8
WINDOW = 512
Q_BLOCK = 128
N_BRANCH = 3
ROPE_THETA = 10000.0
A_WIDTH = A_HEADS * HEAD_DIM
KV_COLS = A_KV_HEADS * HEAD_DIM
B_WIDTH = D_MODEL // 2
R_WIDTH = D_MODEL
R_BLOCKS = 8
R_BLOCK_DIM = R_WIDTH // R_BLOCKS
R_CONV = 4
LRU_C = 8.0
D_FF = 2816
NORM_EPS = 1e-6

LANE = 128
SUBLANE = 8
V7X_VMEM_BYTES = 64 * 1024 * 1024
VMEM_LIMIT = (V7X_VMEM_BYTES * 7) // 8

ROW_TILE = 512
KEY_CHUNK = 256
FF_CHUNK = 512
M_FLOOR = -1e30
LOG2E = 1.4426950408889634

_L = np.arange(LANE)
_IL_HEAD = (_L // 32) % 2
_IL_DIM = (_L % 32) + 32 * (_L // 64)
_ST_HEAD = _L // HEAD_DIM
_ST_DIM = _L % HEAD_DIM
_K_PERM = _IL_HEAD * HEAD_DIM + _IL_DIM
_Q_PERM = np.concatenate([(_IL_HEAD * A_GROUP + g) * HEAD_DIM + _IL_DIM for g in range(A_GROUP)])
_G_PERM = np.array([hk * A_GROUP * N_BRANCH + g * N_BRANCH + br
                    for br in range(N_BRANCH) for hk in range(A_KV_HEADS) for g in range(A_GROUP)])

_C_Q = 0
_C_KCS = 512
_C_KS = 640
_C_KW = 768
_C_VCS = 896
_C_VS = 1024
_C_VW = 1152
_C_GATE = 1280
_C_BG = 1408
_C_CG = 1920
_C_XG = 2432
_C_END = 2944


def _resident(shape):
    nd = len(shape)
    return pl.BlockSpec(shape, lambda *_: (0,) * nd, pipeline_mode=pl.Buffered(1))


def _params(n_axes):
    return pltpu.CompilerParams(dimension_semantics=("arbitrary",) * n_axes,
                                vmem_limit_bytes=VMEM_LIMIT)


def _rms(x, g):
    return (x * lax.rsqrt(jnp.mean(x * x, axis=-1, keepdims=True) + NORM_EPS)) * g


def _gelu(x):
    return 0.5 * x * (1.0 + jnp.tanh(0.7978845608028654 * (x + 0.044715 * (x * x * x))))


def _dot(a, b):
    return jnp.dot(a, b, preferred_element_type=F32)


def _dot_nt(a, b):
    return lax.dot_general(a, b, (((1,), (1,)), ((), ())), preferred_element_type=F32)


def _dot_tn(a, b):
    return lax.dot_general(a, b, (((0,), (0,)), ((), ())), preferred_element_type=F32)


def _even_in_kernel(x_ref, g_ref, w_ref, cos_ref, sin_ref, cw_ref,
                    q_ref, kcs_ref, ks_ref, kw_ref, vcs_ref, vst_ref, vwt_ref, gate_ref, ob_ref,
                    cbuf, *, tiles_per_batch):
    tm = x_ref.shape[0]
    i = pl.program_id(0)
    hb = _rms(x_ref[...], g_ref[...]).astype(BF16)
    cos = cos_ref[...]
    sin = sin_ref[...]

    def rope(z):
        return z * cos + pltpu.roll(z, LANE // 2, 1) * sin

    zq = _dot(hb, w_ref[:, _C_Q:_C_KCS])
    for g in range(A_GROUP):
        q_ref[:, g * LANE:(g + 1) * LANE] = rope(zq[:, g * LANE:(g + 1) * LANE]).astype(BF16)

    zk = _dot(hb, w_ref[:, _C_KCS:_C_VS])
    kcs_ref[...] = rope(zk[:, 0:LANE]).astype(BF16)
    ks_ref[...] = rope(zk[:, LANE:2 * LANE]).astype(BF16)
    kw_ref[...] = rope(zk[:, 2 * LANE:3 * LANE]).astype(BF16)
    vcs_ref[...] = zk[:, 3 * LANE:4 * LANE].astype(BF16)

    zv = _dot(hb, w_ref[:, _C_VS:_C_BG])
    for c in range(tm // KEY_CHUNK):
        rows = slice(c * KEY_CHUNK, (c + 1) * KEY_CHUNK)
        vst_ref[0, c] = zv[rows, 0:LANE].T.astype(BF16)
        vwt_ref[0, c] = zv[rows, LANE:2 * LANE].T.astype(BF16)
    gate_ref[...] = jax.nn.sigmoid(zv[:, 2 * LANE:3 * LANE])

    @pl.when(i % tiles_per_batch == 0)
    def _():
        cbuf[0:SUBLANE, :] = jnp.zeros((SUBLANE, B_WIDTH), F32)

    c0 = _dot(hb, w_ref[:, _C_CG:_C_XG]) * _dot(hb, w_ref[:, _C_XG:_C_END])
    cbuf[SUBLANE:SUBLANE + tm, :] = c0
    cw = cw_ref[...]
    y = (cw[0:1, :] * cbuf[pl.ds(SUBLANE - 2, tm), :]
         + cw[1:2, :] * cbuf[pl.ds(SUBLANE - 1, tm), :]
         + cw[2:3, :] * c0)
    cbuf[0:SUBLANE, :] = c0[tm - SUBLANE:tm, :]
    ob_ref[...] = (_dot(hb, w_ref[:, _C_BG:_C_CG]) * y).astype(BF16)


def _even_in_proj(x2, g, wcat, cos, sin, conv_w, batch, seq):
    n = x2.shape[0]
    tm = ROW_TILE
    tpb = seq // tm
    nch = seq // KEY_CHUNK
    row = lambda w: pl.BlockSpec((tm, w), lambda i: (i, 0))
    vt_spec = pl.BlockSpec((1, tm // KEY_CHUNK, LANE, KEY_CHUNK), lambda i: (i // tpb, i % tpb, 0, 0))
    tab_spec = pl.BlockSpec((tm, LANE), lambda i: (i % tpb, 0))
    out_shape = (
        jax.ShapeDtypeStruct((n, A_WIDTH), BF16),
        jax.ShapeDtypeStruct((n, LANE), BF16),
        jax.ShapeDtypeStruct((n, LANE), BF16),
        jax.ShapeDtypeStruct((n, LANE), BF16),
        jax.ShapeDtypeStruct((n, LANE), BF16),
        jax.ShapeDtypeStruct((batch, nch, LANE, KEY_CHUNK), BF16),
        jax.ShapeDtypeStruct((batch, nch, LANE, KEY_CHUNK), BF16),
        jax.ShapeDtypeStruct((n, LANE), F32),
        jax.ShapeDtypeStruct((n, B_WIDTH), BF16),
    )
    return pl.pallas_call(
        functools.partial(_even_in_kernel, tiles_per_batch=tpb),
        out_shape=out_shape,
        grid=(n // tm,),
        in_specs=[row(D_MODEL), _resident((1, D_MODEL)), _resident((D_MODEL, _C_END)),
                  tab_spec, tab_spec, _resident((3, B_WIDTH))],
        out_specs=(row(A_WIDTH), row(LANE), row(LANE), row(LANE), row(LANE),
                   vt_spec, vt_spec, row(LANE), row(B_WIDTH)),
        scratch_shapes=[pltpu.VMEM((tm + SUBLANE, B_WIDTH), F32)],
        compiler_params=_params(1),
        name="even_in_proj",
    )(x2, g, wcat, cos, sin, conv_w)


def _compress_kernel(k2_ref, v2_ref, w1k_ref, w2k_ref, pek_ref, w1v_ref, w2v_ref, pev_ref,
                     kc_ref, vct_ref, sbuf):
    ncp = k2_ref.shape[1]

    def one(x2, w1_ref, w2_ref, pe_ref):
        top = _dot(x2, w1_ref[0])
        sbuf[0:ncp, :] = _dot(x2, w1_ref[1])
        sbuf[ncp:ncp + SUBLANE, :] = jnp.zeros((SUBLANE, sbuf.shape[1]), F32)
        bias = _dot(pe_ref[0], w1_ref[0]) + _dot(pe_ref[1], w1_ref[1])
        hid = top + sbuf[pl.ds(1, ncp), :] + bias[0:1, :]
        return _dot(_gelu(hid).astype(BF16), w2_ref[...])

    kc_ref[0] = one(k2_ref[0], w1k_ref, w2k_ref, pek_ref).astype(BF16)
    vct_ref[0] = one(v2_ref[0], w1v_ref, w2v_ref, pev_ref).T.astype(BF16)


def _compress(k2, v2, w1k, w2k, pek, w1v, w2v, pev):
    batch, ncp, width = k2.shape
    hid2 = A_KV_HEADS * CMP_HIDDEN
    src = pl.BlockSpec((1, ncp, width), lambda b: (b, 0, 0))
    return pl.pallas_call(
        _compress_kernel,
        out_shape=(jax.ShapeDtypeStruct((batch, ncp, LANE), BF16),
                   jax.ShapeDtypeStruct((batch, LANE, ncp), BF16)),
        grid=(batch,),
        in_specs=[src, src,
                  _resident((2, width, hid2)), _resident((hid2, LANE)), _resident((2, SUBLANE, width)),
                  _resident((2, width, hid2)), _resident((hid2, LANE)), _resident((2, SUBLANE, width))],
        out_specs=(pl.BlockSpec((1, ncp, LANE), lambda b: (b, 0, 0)),
                   pl.BlockSpec((1, LANE, ncp), lambda b: (b, 0, 0))),
        scratch_shapes=[pltpu.VMEM((ncp + SUBLANE, hid2), F32)],
        compiler_params=_params(1),
        name="compress",
    )(k2, v2, w1k, w2k, pek, w1v, w2v, pev)


_SLC, _WIN = 0, 1
_ONES_ROWS = 16


def _attn_kernel(q_ref, kc_ref, vct_ref, ks_ref, vst_ref, kw_ref, vwt_ref, gate_ref, blk_ref, o_ref,
                 ps_ref, sel_ref, rank_ref, gt_ref, m_ref, acc_ref, snx_ref, pdf_ref, adf_ref,
                 *, n_sel):
    qb = pl.program_id(1)
    ncp = kc_ref.shape[1]
    per = SLC_BLOCK // CMP_STRIDE
    nb = ncp // per
    nq = A_GROUP * Q_BLOCK
    blocks_per_chunk = KEY_CHUNK // SLC_BLOCK
    t0 = qb * Q_BLOCK
    last = qb // (KEY_CHUNK // Q_BLOCK)
    n_pairs = last // 2
    s0, s1 = _SLC * A_KV_HEADS, _SLC * A_KV_HEADS + 1

    lane = lax.broadcasted_iota(jnp.int32, (Q_BLOCK, LANE), 1)
    qmats = []
    for hk in range(A_KV_HEADS):
        head_mask = jnp.where((lane // 32) % 2 == hk, 1.0, 0.0).astype(BF16)
        qmats.append(jnp.concatenate(
            [q_ref[:, g * LANE:(g + 1) * LANE] * head_mask for g in range(A_GROUP)], axis=0))
    q_t = [qm.T for qm in qmats]
    pos = t0 + lax.broadcasted_iota(jnp.int32, (1, nq), 1) % Q_BLOCK
    chunk_rows = lax.broadcasted_iota(jnp.int32, (KEY_CHUNK, 1), 0)
    ones_rows = jnp.ones((_ONES_ROWS, KEY_CHUNK), BF16)

    def key_chunk(k_ref, c):
        return k_ref[0, pl.ds(pl.multiple_of(c * KEY_CHUNK, KEY_CHUNK), KEY_CHUNK), :]

    def vt_of(vt_ref, c, hk):
        return jnp.concatenate([vt_ref[0, c, hk * HEAD_DIM:(hk + 1) * HEAD_DIM, :], ones_rows], axis=0)

    def softmax_update(slot, s):
        m_old = m_ref[slot]
        m_new = jnp.maximum(m_old, jnp.max(s, axis=0, keepdims=True))
        m_ref[slot] = m_new
        return jnp.exp2(s - m_new).astype(BF16), jnp.exp2(m_old - m_new)

    def pv_update(slot, vt, p, alpha):
        acc_ref[slot] = alpha * acc_ref[slot] + _dot(vt, p)

    def causal(s, c):
        return jnp.where(chunk_rows <= pos - c * KEY_CHUNK, s, -jnp.inf)

    snx_ref[...] = _dot(key_chunk(ks_ref, 0), q_t[0])
    m_ref[...] = jnp.full(m_ref.shape, M_FLOOR, F32)
    acc_ref[...] = jnp.zeros(acc_ref.shape, F32)
    pdf_ref[...] = jnp.zeros(pdf_ref.shape, BF16)
    adf_ref[...] = jnp.ones(adf_ref.shape, F32)

    cend = lax.broadcasted_iota(jnp.int32, (ncp, 1), 0) * CMP_STRIDE + (CMP_LEN - 1)
    bidx = lax.broadcasted_iota(jnp.int32, (nb, Q_BLOCK), 0)
    cur = (t0 + lax.broadcasted_iota(jnp.int32, (1, Q_BLOCK), 1)) // SLC_BLOCK
    forced = (bidx == 0) | (bidx == cur) | (bidx == cur - 1)
    o_cmp = []
    sc_all = [jnp.where(cend <= pos, _dot(kc_ref[0], q_t[hk]), -jnp.inf)
              for hk in range(A_KV_HEADS)]
    for hk in range(A_KV_HEADS):
        sc = sc_all[hk]
        m_c = jnp.maximum(jnp.max(sc, axis=0, keepdims=True), M_FLOOR)
        p_c = jnp.exp2(sc - m_c)
        pn = p_c * (1.0 / jnp.maximum(jnp.sum(p_c, axis=0, keepdims=True), 1e-30))
        o_cmp.append(_dot(vct_ref[0, hk * HEAD_DIM:(hk + 1) * HEAD_DIM, :], pn.astype(BF16)))
        psum = pn[:, 0:Q_BLOCK]
        for g in range(1, A_GROUP):
            psum = psum + pn[:, g * Q_BLOCK:(g + 1) * Q_BLOCK]
        ps_ref[hk, 0:SUBLANE, :] = jnp.zeros((SUBLANE, Q_BLOCK), F32)
        ps_ref[hk, SUBLANE:SUBLANE + ncp, :] = psum
        imp = ps_ref[hk, pl.ds(SUBLANE - 1, nb, stride=per), :]
        for k in range(per):
            imp = imp + ps_ref[hk, pl.ds(SUBLANE + k, nb, stride=per), :]
        sel_ref[hk, 0:nb, :] = jnp.where(forced, 1e4, jnp.where(bidx <= cur, imp, -1e4))

    rank_ref[...] = jnp.zeros(rank_ref.shape, F32)
    rows8 = lax.broadcasted_iota(jnp.int32, (SUBLANE, Q_BLOCK), 0)
    n_slab = nb // SUBLANE
    for grp in range(n_slab):
        @pl.when(grp * SUBLANE <= t0 // SLC_BLOCK + 1)
        def _():
            for hk in range(A_KV_HEADS):
                slabs = [sel_ref[hk, r * SUBLANE:(r + 1) * SUBLANE, :] for r in range(n_slab)]
                ranks = [rank_ref[hk, r * SUBLANE:(r + 1) * SUBLANE, :] for r in range(n_slab)]
                for k in range(grp * SUBLANE, (grp + 1) * SUBLANE):
                    rk = sel_ref[hk, k:k + 1, :]
                    for r in range(n_slab):
                        if r * SUBLANE > k:
                            beats = rk >= slabs[r]
                        elif (r + 1) * SUBLANE <= k:
                            beats = rk > slabs[r]
                        else:
                            beats = (rk > slabs[r]) | ((rk == slabs[r]) & (rows8 + r * SUBLANE > k))
                        ranks[r] = ranks[r] + jnp.where(beats, 1.0, 0.0)
                for r in range(n_slab):
                    rank_ref[hk, r * SUBLANE:(r + 1) * SUBLANE, :] = ranks[r]

    q_aug = []
    for hk in range(A_KV_HEADS):
        bias = jnp.where(rank_ref[hk] < n_sel, 0.0, M_FLOOR)
        sel_ref[hk, 0:nb, :] = bias
        if nb < LANE:
            sel_ref[hk, nb:LANE, :] = jnp.zeros((LANE - nb, Q_BLOCK), F32)
        bias_rows = jnp.concatenate([sel_ref[hk].astype(BF16)] * A_GROUP, axis=1)
        q_aug.append(jnp.concatenate([q_t[hk], bias_rows], axis=0))

    def slc_scores(c, hk):
        k_aug = jnp.concatenate([key_chunk(ks_ref, c), key_chunk(blk_ref, c)], axis=1)
        return _dot(k_aug, q_aug[hk])

    for bl in range(blocks_per_chunk):
        rows = slice(bl * SLC_BLOCK, (bl + 1) * SLC_BLOCK)
        row = jnp.concatenate([sel_ref[0, bl:bl + 1, :]] * A_GROUP, axis=1)
        snx_ref[rows, :] = snx_ref[rows, :] + row

    def pair_body(i, carry):
        ca = 2 * i
        cb = ca + 1
        pv_update(s1, vt_of(vst_ref, jnp.maximum(ca - 1, 0), 1), pdf_ref[...], adf_ref[...])
        sb = slc_scores(ca, 1)
        pa, aa = softmax_update(s0, snx_ref[...])
        pv_update(s0, vt_of(vst_ref, ca, 0), pa, aa)
        sc_ = slc_scores(cb, 0)
        pb, ab = softmax_update(s1, sb)
        pv_update(s1, vt_of(vst_ref, ca, 1), pb, ab)
        sd = slc_scores(cb, 1)
        pc, ac = softmax_update(s0, sc_)
        pv_update(s0, vt_of(vst_ref, cb, 0), pc, ac)
        snx_ref[...] = slc_scores(cb + 1, 0)
        pd, ad = softmax_update(s1, sd)
        pdf_ref[...] = pd
        adf_ref[...] = ad
        return carry

    lax.fori_loop(0, n_pairs, pair_body, 0)

    c0 = 2 * n_pairs
    pv_update(s1, vt_of(vst_ref, jnp.maximum(c0 - 1, 0), 1), pdf_ref[...], adf_ref[...])

    def slc_tail(c, hk, prefetched):
        def score():
            return causal(snx_ref[...] if prefetched else slc_scores(c, hk), c)
        return (_SLC * A_KV_HEADS + hk, score, lambda: vt_of(vst_ref, c, hk))

    def win_tail(c, hk, diag):
        cc = jnp.maximum(c, 0)

        def score():
            s = _dot(key_chunk(kw_ref, cc), q_t[hk])
            if diag:
                return causal(s, c)
            low = jnp.where(c >= 0, pos - WINDOW - c * KEY_CHUNK, KEY_CHUNK)
            return jnp.where(chunk_rows > low, s, -jnp.inf)
        return (_WIN * A_KV_HEADS + hk, score, lambda: vt_of(vwt_ref, cc, hk))

    units = [slc_tail(c0, 0, True), slc_tail(c0, 1, False),
             win_tail(last - 2, 0, False), win_tail(last - 2, 1, False),
             slc_tail(c0 + 1, 0, False), slc_tail(c0 + 1, 1, False),
             win_tail(last - 1, 0, False), win_tail(last - 1, 1, False),
             win_tail(last, 0, True), win_tail(last, 1, True)]
    s_next = units[0][1]()
    for i, (slot, _, vt_fn) in enumerate(units):
        s = s_next
        if i + 1 < len(units):
            s_next = units[i + 1][1]()
        p, alpha = softmax_update(slot, s)
        pv_update(slot, vt_fn(), p, alpha)

    gt_ref[...] = gate_ref[...].T
    for hk in range(A_KV_HEADS):
        def gate(br):
            base = br * A_HEADS + hk * A_GROUP
            return jnp.concatenate([gt_ref[base + g:base + g + 1, :] for g in range(A_GROUP)], axis=1)

        def branch(slot):
            acc = acc_ref[slot]
            return acc[0:HEAD_DIM, :] * (1.0 / jnp.maximum(acc[HEAD_DIM:HEAD_DIM + 1, :], 1e-30))

        o = (gate(0) * o_cmp[hk] + gate(1) * branch(_SLC * A_KV_HEADS + hk)
             + gate(2) * branch(_WIN * A_KV_HEADS + hk))
        for g in range(A_GROUP):
            r0 = (hk * A_GROUP + g) * HEAD_DIM
            o_ref[0, r0:r0 + HEAD_DIM, :] = o[:, g * Q_BLOCK:(g + 1) * Q_BLOCK].astype(BF16)


def _attention(q, kc, vct, ks, vst, kw, vwt, gates, batch, seq):
    nqb = seq // Q_BLOCK
    ncp = kc.shape[1]
    nch = seq // KEY_CHUNK
    nb = seq // SLC_BLOCK
    assert nb <= LANE
    nq = A_GROUP * Q_BLOCK
    n_state = 2 * A_KV_HEADS
    ks3 = ks.reshape(batch, seq, LANE)
    kw3 = kw.reshape(batch, seq, LANE)
    block_onehot = (jnp.arange(seq)[:, None] // SLC_BLOCK == jnp.arange(LANE)[None, :]).astype(BF16)
    kspec = pl.BlockSpec((1, seq, LANE), lambda b, t: (b, 0, 0))
    vspec = pl.BlockSpec((1, nch, LANE, KEY_CHUNK), lambda b, t: (b, 0, 0, 0))
    return pl.pallas_call(
        functools.partial(_attn_kernel, n_sel=min(SLC_TOPK, nb)),
        out_shape=jax.ShapeDtypeStruct((batch, A_WIDTH, seq), BF16),
        grid=(batch, nqb),
        in_specs=[pl.BlockSpec((Q_BLOCK, A_WIDTH), lambda b, t: (b * nqb + t, 0)),
                  pl.BlockSpec((1, ncp, LANE), lambda b, t: (b, 0, 0)),
                  pl.BlockSpec((1, LANE, ncp), lambda b, t: (b, 0, 0)),
                  kspec, vspec, kspec, vspec,
                  pl.BlockSpec((Q_BLOCK, LANE), lambda b, t: (b * nqb + t, 0)),
                  _resident((1, seq, LANE))],
        out_specs=pl.BlockSpec((1, A_WIDTH, Q_BLOCK), lambda b, t: (b, 0, t)),
        scratch_shapes=[pltpu.VMEM((A_KV_HEADS, ncp + SUBLANE, Q_BLOCK), F32),
                        pltpu.VMEM((A_KV_HEADS, LANE, Q_BLOCK), F32),
                        pltpu.VMEM((A_KV_HEADS, nb, Q_BLOCK), F32),
                        pltpu.VMEM((LANE, Q_BLOCK), F32),
                        pltpu.VMEM((n_state, 1, nq), F32),
                        pltpu.VMEM((n_state, HEAD_DIM + _ONES_ROWS, nq), F32),
                        pltpu.VMEM((KEY_CHUNK, nq), F32),
                        pltpu.VMEM((KEY_CHUNK, nq), BF16),
                        pltpu.VMEM((1, nq), F32)],
        compiler_params=_params(2),
        name="nsa_attention",
    )(q, kc, vct, ks3, vst, kw3, vwt, gates, block_onehot.reshape(1, seq, LANE))


def _out_proj_kernel(x_ref, ot_ref, ob_ref, wa_ref, wb_ref, o_ref):
    o_ref[...] = x_ref[...] + _dot_tn(ot_ref[0], wa_ref[...]) + _dot(ob_ref[...], wb_ref[...])


def _out_proj(x2, o_t, o_b, wa, wb, seq):
    n = x2.shape[0]
    tm = ROW_TILE
    tpb = seq // tm
    row = lambda w: pl.BlockSpec((tm, w), lambda i: (i, 0))
    return pl.pallas_call(
        _out_proj_kernel,
        out_shape=jax.ShapeDtypeStruct((n, D_MODEL), F32),
        grid=(n // tm,),
        in_specs=[row(D_MODEL), pl.BlockSpec((1, A_WIDTH, tm), lambda i: (i // tpb, 0, i % tpb)),
                  row(B_WIDTH), _resident((A_WIDTH, D_MODEL)), _resident((B_WIDTH, D_MODEL))],
        out_specs=row(D_MODEL),
        compiler_params=_params(1),
        name="even_out_proj",
    )(x2, o_t, o_b, wa, wb)


def _rglru_kernel(x_ref, g_ref, win_ref, bin_ref, cw_ref, cb_ref, wai_ref, ba_ref, bi_ref, lam_ref,
                  wout_ref, bout_ref, o_ref, zbuf, abuf, bbuf, hbuf, hprev, *, tiles_per_batch):
    tm = x_ref.shape[0]
    i = pl.program_id(0)

    @pl.when(i % tiles_per_batch == 0)
    def _():
        zbuf[0:SUBLANE, :] = jnp.zeros((SUBLANE, R_WIDTH), F32)
        hprev[...] = jnp.zeros((SUBLANE, R_WIDTH), F32)

    x = x_ref[...]
    hb = _rms(x, g_ref[...]).astype(BF16)
    y = _gelu(_dot(hb, win_ref[:, 0:R_WIDTH]) + bin_ref[:, 0:R_WIDTH])
    z2 = _dot(hb, win_ref[:, R_WIDTH:2 * R_WIDTH]) + bin_ref[:, R_WIDTH:2 * R_WIDTH]
    zbuf[SUBLANE:SUBLANE + tm, :] = z2
    cw = cw_ref[...]
    u = (cw[0:1, :] * zbuf[pl.ds(SUBLANE - 3, tm), :]
         + cw[1:2, :] * zbuf[pl.ds(SUBLANE - 2, tm), :]
         + cw[2:3, :] * zbuf[pl.ds(SUBLANE - 1, tm), :]
         + cw[3:4, :] * z2 + cb_ref[...])
    zbuf[0:SUBLANE, :] = z2[tm - SUBLANE:tm, :]

    ub = u.astype(BF16)
    log_sig = jax.nn.log_sigmoid(lam_ref[...])
    for n in range(R_BLOCKS):
        cols = slice(n * R_BLOCK_DIM, (n + 1) * R_BLOCK_DIM)
        ri = _dot(ub[:, cols], wai_ref[n])
        r = jax.nn.sigmoid(ri[:, 0:R_BLOCK_DIM] + ba_ref[:, cols])
        gi = jax.nn.sigmoid(ri[:, R_BLOCK_DIM:2 * R_BLOCK_DIM] + bi_ref[:, cols])
        log_a = (LRU_C * log_sig[:, cols]) * r
        a = jnp.exp(log_a)
        mult = jnp.sqrt(-jnp.tanh(log_a) * (a * a + 1.0))
        abuf[:, cols] = a
        bbuf[:, cols] = mult * (gi * u[:, cols])

    rows = lax.broadcasted_iota(jnp.int32, (SUBLANE, R_WIDTH), 0)

    def slab(j, hp):
        r0 = pl.multiple_of(j * SUBLANE, SUBLANE)
        a = abuf[pl.ds(r0, SUBLANE), :]
        b = bbuf[pl.ds(r0, SUBLANE), :]
        for d in (1, 2, 4):
            keep = rows >= d
            b = jnp.where(keep, a * pltpu.roll(b, d, 0) + b, b)
            a = jnp.where(keep, a * pltpu.roll(a, d, 0), a)
        h = a * hp + b
        hbuf[pl.ds(r0, SUBLANE), :] = h
        return jnp.broadcast_to(h[SUBLANE - 1:SUBLANE, :], (SUBLANE, R_WIDTH))

    hprev[...] = lax.fori_loop(0, tm // SUBLANE, slab, hprev[...])
    o_ref[...] = x + _dot((hbuf[...] * y).astype(BF16), wout_ref[...]) + bout_ref[...]


def _rglru(x2, g, win, b_in, conv_w, conv_b, wai, b_a, b_i, lam, wout, b_out, seq):
    n = x2.shape[0]
    tm = ROW_TILE
    tpb = seq // tm
    row = pl.BlockSpec((tm, D_MODEL), lambda i: (i, 0))
    buf = lambda r: pltpu.VMEM((r, R_WIDTH), F32)
    return pl.pallas_call(
        functools.partial(_rglru_kernel, tiles_per_batch=tpb),
        out_shape=jax.ShapeDtypeStruct((n, D_MODEL), F32),
        grid=(n // tm,),
        in_specs=[row, _resident((1, D_MODEL)), _resident((D_MODEL, 2 * R_WIDTH)), _resident((1, 2 * R_WIDTH)),
                  _resident((R_CONV, R_WIDTH)), _resident((1, R_WIDTH)),
                  _resident((R_BLOCKS, R_BLOCK_DIM, 2 * R_BLOCK_DIM)),
                  _resident((1, R_WIDTH)), _resident((1, R_WIDTH)), _resident((1, R_WIDTH)),
                  _resident((R_WIDTH, D_MODEL)), _resident((1, D_MODEL))],
        out_specs=row,
        scratch_shapes=[buf(tm + SUBLANE), buf(tm), buf(tm), buf(tm), buf(SUBLANE)],
        compiler_params=_params(1),
        name="rglru_mixer",
    )(x2, g, win, b_in, conv_w, conv_b, wai, b_a, b_i, lam, wout, b_out)


def _ffn_kernel(x_ref, g_ref, wup_ref, cw_ref, cb_ref, wdn_ref, gf_ref, o_ref, ubuf, carry, acc,
                *, tiles_per_batch, final_norm):
    tm = x_ref.shape[0]
    i = pl.program_id(0)

    @pl.when(i % tiles_per_batch == 0)
    def _():
        carry[...] = jnp.zeros(carry.shape, F32)

    x = x_ref[...]
    hb = _rms(x, g_ref[...]).astype(BF16)
    acc[...] = x
    for c0 in range(0, D_FF, FF_CHUNK):
        width = min(FF_CHUNK, D_FF - c0)
        branch = []
        for half in range(2):
            col = half * D_FF + c0
            cols = slice(col, col + width)
            u = _dot(hb, wup_ref[:, cols])
            ubuf[half, 0:SUBLANE, 0:width] = carry[:, cols]
            ubuf[half, SUBLANE:SUBLANE + tm, 0:width] = u
            carry[:, cols] = u[tm - SUBLANE:tm, :]
            w = cw_ref[:, cols]
            branch.append(w[0:1, :] * ubuf[half, pl.ds(SUBLANE - 2, tm), 0:width]
                          + w[1:2, :] * ubuf[half, pl.ds(SUBLANE - 1, tm), 0:width]
                          + w[2:3, :] * u + cb_ref[:, cols])
        act = (branch[0] * jax.nn.sigmoid(branch[0]) * branch[1]).astype(BF16)
        acc[...] += _dot(act, wdn_ref[c0:c0 + width, :])
    out = acc[...]
    if final_norm:
        out = _rms(out, gf_ref[...])
    o_ref[...] = out


def _ffn(x2, g, wup, conv_w, conv_b, wdn, g_final, seq, final_norm):
    n = x2.shape[0]
    tm = ROW_TILE
    tpb = seq // tm
    row = pl.BlockSpec((tm, D_MODEL), lambda i: (i, 0))
    return pl.pallas_call(
        functools.partial(_ffn_kernel, tiles_per_batch=tpb, final_norm=final_norm),
        out_shape=jax.ShapeDtypeStruct((n, D_MODEL), F32),
        grid=(n // tm,),
        in_specs=[row, _resident((1, D_MODEL)), _resident((D_MODEL, 2 * D_FF)), _resident((3, 2 * D_FF)),
                  _resident((1, 2 * D_FF)), _resident((D_FF, D_MODEL)), _resident((1, D_MODEL))],
        out_specs=row,
        scratch_shapes=[pltpu.VMEM((2, tm + SUBLANE, FF_CHUNK), F32),
                        pltpu.VMEM((SUBLANE, 2 * D_FF), F32),
                        pltpu.VMEM((tm, D_MODEL), F32)],
        compiler_params=_params(1),
        name="conv_ffn",
    )(x2, g, wup, conv_w, conv_b, wdn, g_final)


def _rope_tables(seq):
    inv = 1.0 / (ROPE_THETA ** (jnp.arange(0, HEAD_DIM, 2, dtype=F32) / HEAD_DIM))
    ang = jnp.arange(seq, dtype=F32)[:, None] * inv[None, :]
    ang = jnp.tile(ang, (1, LANE // (HEAD_DIM // 2)))
    sign = jnp.where(jnp.arange(LANE) < LANE // 2, -1.0, 1.0).astype(F32)
    return jnp.cos(ang), jnp.sin(ang) * sign[None, :]


def _even_in_weights(w_in):
    kv0 = A_WIDTH
    part = lambda p: w_in[:, kv0 + p * KV_COLS: kv0 + (p + 1) * KV_COLS]
    g0 = kv0 + 6 * KV_COLS
    bc0 = g0 + A_HEADS * N_BRANCH
    scale = (HEAD_DIM ** -0.5) * LOG2E
    gates = jnp.pad(w_in[:, g0:bc0][:, _G_PERM], ((0, 0), (0, LANE - A_HEADS * N_BRANCH)))
    return jnp.concatenate([
        w_in[:, :A_WIDTH][:, _Q_PERM] * scale,
        part(0)[:, _K_PERM], part(2)[:, _K_PERM], part(4)[:, _K_PERM],
        part(1), part(3), part(5),
        gates,
        w_in[:, bc0:],
    ], axis=1).astype(BF16)


def _compress_weights(pe, w1, w2, lane_head, lane_dim):
    onehot = jnp.asarray(lane_head[:, None] == np.arange(A_KV_HEADS)[None, :], F32)
    w1g = w1.reshape(CMP_LEN, HEAD_DIM, CMP_HIDDEN)[:, lane_dim, :]
    w1x = (w1g[:, :, None, :] * onehot[None, :, :, None]).reshape(CMP_LEN * LANE, A_KV_HEADS * CMP_HIDDEN)
    half = (CMP_LEN // 2) * LANE
    w1x = jnp.stack([w1x[:half], w1x[half:]]).astype(BF16)
    w2x = (onehot.T[:, None, :] * w2[:, lane_dim][None, :, :]).reshape(A_KV_HEADS * CMP_HIDDEN, LANE)
    pel = pe[:, lane_dim].reshape(2, 1, half)
    pex = jnp.broadcast_to(pel, (2, SUBLANE, half)).astype(BF16)
    return w1x, w2x.astype(BF16), pex


def kernel(x, norm_mix, norm_ffn, norm_final, a_w_in, a_cmp_pe, a_cmp_w1, a_cmp_w2, a_conv_w, a_w_out,
           c_w_in, c_b_in, c_conv_w, c_conv_b, c_w_a, c_b_a, c_w_i, c_b_i, c_lambda, c_w_out, c_b_out,
           f_w_up, f_conv_w, f_conv_b, f_w_down):
    batch, seq, _ = x.shape
    assert seq % ROW_TILE == 0 and ROW_TILE % KEY_CHUNK == 0 and seq // SLC_BLOCK >= 1
    x2 = x.reshape(batch * seq, D_MODEL)
    cos, sin = _rope_tables(seq)
    row = lambda v: v.reshape(1, -1)
    for layer in range(DEPTH):
        j = layer // 2
        g_mix = row(norm_mix[layer])
        if layer % 2 == 0:
            wcat = _even_in_weights(a_w_in[j])
            q, kcs, ks, kw, vcs, vst, vwt, gates, o_b = _even_in_proj(
                x2, g_mix, wcat, cos, sin, a_conv_w[j], batch, seq)
            ncp = seq // CMP_STRIDE
            w1k, w2k, pek = _compress_weights(a_cmp_pe[j, 0], a_cmp_w1[j, 0], a_cmp_w2[j, 0], _IL_HEAD, _IL_DIM)
            w1v, w2v, pev = _compress_weights(a_cmp_pe[j, 1], a_cmp_w1[j, 1], a_cmp_w2[j, 1], _ST_HEAD, _ST_DIM)
            kc, vct = _compress(kcs.reshape(batch, ncp, CMP_STRIDE * LANE),
                                vcs.reshape(batch, ncp, CMP_STRIDE * LANE),
                                w1k, w2k, pek, w1v, w2v, pev)
            o_t = _attention(q, kc, vct, ks, vst, kw, vwt, gates, batch, seq)
            w_out = a_w_out[j].astype(BF16)
            x2 = _out_proj(x2, o_t, o_b, w_out[:A_WIDTH], w_out[A_WIDTH:], seq)
        else:
            wai = jnp.concatenate([c_w_a[j], c_w_i[j]], axis=-1).astype(BF16)
            x2 = _rglru(x2, g_mix, c_w_in[j].astype(BF16), row(c_b_in[j]), c_conv_w[j], row(c_conv_b[j]),
                        wai, row(c_b_a[j]), row(c_b_i[j]), row(c_lambda[j]),
                        c_w_out[j].astype(BF16), row(c_b_out[j]), seq)
        x2 = _ffn(x2, row(norm_ffn[layer]), f_w_up[layer].astype(BF16), f_conv_w[layer],
                  row(f_conv_b[layer]), f_w_down[layer].astype(BF16), row(norm_final), seq,
                  final_norm=(layer == DEPTH - 1))
    return x2.reshape(batch, seq, D_MODEL)
```

```python
import functools
import math

import numpy as np
import jax
import jax.numpy as jnp
from jax import lax
from jax.experimental import pallas as pl
from jax.experimental.pallas import tpu as pltpu

F32 = jnp.float32
BF16 = jnp.bfloat16

D_MODEL = 1024
DEPTH = 4
A_HEADS = 8
A_KV_HEADS = 2
A_GROUP = A_HEADS // A_KV_HEADS
HEAD_DIM = 64
CMP_LEN = 32
CMP_STRIDE = 16
CMP_HIDDEN = 128
SLC_BLOCK = 64
SLC_TOPK = 8
WINDOW = 512
Q_BLOCK = 128
N_BRANCH = 3
ROPE_THETA = 10000.0
A_WIDTH = A_HEADS * HEAD_DIM
KV_COLS = A_KV_HEADS * HEAD_DIM
B_WIDTH = D_MODEL // 2
R_WIDTH = D_MODEL
R_BLOCKS = 8
R_BLOCK_DIM = R_WIDTH // R_BLOCKS
R_CONV = 4
LRU_C = 8.0
D_FF = 2816
NORM_EPS = 1e-6

LANE = 128
SUBLANE = 8
V7X_VMEM_BYTES = 64 * 1024 * 1024
VMEM_LIMIT = (V7X_VMEM_BYTES * 7) // 8

ROW_TILE = 512
KEY_CHUNK = 256
FF_CHUNK = 512
M_FLOOR = -1e30
LOG2E = 1.4426950408889634

_L = np.arange(LANE)
_IL_HEAD = (_L // 32) % 2
_IL_DIM = (_L % 32) + 32 * (_L // 64)
_ST_HEAD = _L // HEAD_DIM
_ST_DIM = _L % HEAD_DIM
_K_PERM = _IL_HEAD * HEAD_DIM + _IL_DIM
_Q_PERM = np.concatenate([(_IL_HEAD * A_GROUP + g) * HEAD_DIM + _IL_DIM for g in range(A_GROUP)])
_G_PERM = np.array([hk * A_GROUP * N_BRANCH + g * N_BRANCH + br
                    for br in range(N_BRANCH) for hk in range(A_KV_HEADS) for g in range(A_GROUP)])

_C_Q = 0
_C_KCS = 512
_C_KS = 640
_C_KW = 768
_C_VCS = 896
_C_VS = 1024
_C_VW = 1152
_C_GATE = 1280
_C_BG = 1408
_C_CG = 1920
_C_XG = 2432
_C_END = 2944


def _resident(shape):
    nd = len(shape)
    return pl.BlockSpec(shape, lambda *_: (0,) * nd, pipeline_mode=pl.Buffered(1))


def _params(n_axes):
    return pltpu.CompilerParams(dimension_semantics=("arbitrary",) * n_axes,
                                vmem_limit_bytes=VMEM_LIMIT)


def _rms(x, g):
    return (x * lax.rsqrt(jnp.mean(x * x, axis=-1, keepdims=True) + NORM_EPS)) * g


def _gelu(x):
    return 0.5 * x * (1.0 + jnp.tanh(0.7978845608028654 * (x + 0.044715 * (x * x * x))))


def _dot(a, b):
    return jnp.dot(a, b, preferred_element_type=F32)


def _even_in_kernel(x_ref, g_ref, w_ref, cos_ref, sin_ref, cw_ref,
                    q_ref, kcs_ref, ks_ref, kw_ref, vcs_ref, vst_ref, vwt_ref, gate_ref, ob_ref,
                    cbuf, *, tiles_per_batch):
    tm = x_ref.shape[0]
    i = pl.program_id(0)
    hb = _rms(x_ref[...], g_ref[...]).astype(BF16)
    cos = cos_ref[...]
    sin = sin_ref[...]

    def rope(z):
        return z * cos + pltpu.roll(z, LANE // 2, 1) * sin

    zq = _dot(hb, w_ref[:, _C_Q:_C_KCS])
    for g in range(A_GROUP):
        q_ref[:, g * LANE:(g + 1) * LANE] = rope(zq[:, g * LANE:(g + 1) * LANE]).astype(BF16)

    zk = _dot(hb, w_ref[:, _C_KCS:_C_VS])
    kcs_ref[...] = rope(zk[:, 0:LANE]).astype(BF16)
    ks_ref[...] = rope(zk[:, LANE:2 * LANE]).astype(BF16)
    kw_ref[...] = rope(zk[:, 2 * LANE:3 * LANE]).astype(BF16)
    vcs_ref[...] = zk[:, 3 * LANE:4 * LANE].astype(BF16)

    zv = _dot(hb, w_ref[:, _C_VS:_C_BG])
    for c in range(tm // KEY_CHUNK):
        rows = slice(c * KEY_CHUNK, (c + 1) * KEY_CHUNK)
        vst_ref[0, c] = zv[rows, 0:LANE].T.astype(BF16)
        vwt_ref[0, c] = zv[rows, LANE:2 * LANE].T.astype(BF16)
    gate_ref[...] = jax.nn.sigmoid(zv[:, 2 * LANE:3 * LANE])

    @pl.when(i % tiles_per_batch == 0)
    def _():
        cbuf[0:SUBLANE, :] = jnp.zeros((SUBLANE, B_WIDTH), F32)

    c0 = _dot(hb, w_ref[:, _C_CG:_C_XG]) * _dot(hb, w_ref[:, _C_XG:_C_END])
    cbuf[SUBLANE:SUBLANE + tm, :] = c0
    cw = cw_ref[...]
    y = (cw[0:1, :] * cbuf[pl.ds(SUBLANE - 2, tm), :]
         + cw[1:2, :] * cbuf[pl.ds(SUBLANE - 1, tm), :]
         + cw[2:3, :] * c0)
    cbuf[0:SUBLANE, :] = c0[tm - SUBLANE:tm, :]
    ob_ref[...] = (_dot(hb, w_ref[:, _C_BG:_C_CG]) * y).astype(BF16)


def _even_in_proj(x2, g, wcat, cos, sin, conv_w, batch, seq):
    n = x2.shape[0]
    tm = ROW_TILE
    tpb = seq // tm
    nch = seq // KEY_CHUNK
    row = lambda w: pl.BlockSpec((tm, w), lambda i: (i, 0))
    vt_spec = pl.BlockSpec((1, tm // KEY_CHUNK, LANE, KEY_CHUNK), lambda i: (i // tpb, i % tpb, 0, 0))
    tab_spec = pl.BlockSpec((tm, LANE), lambda i: (i % tpb, 0))
    out_shape = (
        jax.ShapeDtypeStruct((n, A_WIDTH), BF16),
        jax.ShapeDtypeStruct((n, LANE), BF16),
        jax.ShapeDtypeStruct((n, LANE), BF16),
        jax.ShapeDtypeStruct((n, LANE), BF16),
        jax.ShapeDtypeStruct((n, LANE), BF16),
        jax.ShapeDtypeStruct((batch, nch, LANE, KEY_CHUNK), BF16),
        jax.ShapeDtypeStruct((batch, nch, LANE, KEY_CHUNK), BF16),
        jax.ShapeDtypeStruct((n, LANE), F32),
        jax.ShapeDtypeStruct((n, B_WIDTH), BF16),
    )
    return pl.pallas_call(
        functools.partial(_even_in_kernel, tiles_per_batch=tpb),
        out_shape=out_shape,
        grid=(n // tm,),
        in_specs=[row(D_MODEL), _resident((1, D_MODEL)), _resident((D_MODEL, _C_END)),
                  tab_spec, tab_spec, _resident((3, B_WIDTH))],
        out_specs=(row(A_WIDTH), row(LANE), row(LANE), row(LANE), row(LANE),
                   vt_spec, vt_spec, row(LANE), row(B_WIDTH)),
        scratch_shapes=[pltpu.VMEM((tm + SUBLANE, B_WIDTH), F32)],
        compiler_params=_params(1),
        name="even_in_proj",
    )(x2, g, wcat, cos, sin, conv_w)


def _compress_kernel(k2_ref, v2_ref, w1k_ref, w2k_ref, pek_ref, w1v_ref, w2v_ref, pev_ref,
                     kc_ref, vct_ref, sbuf):
    ncp = k2_ref.shape[1]

    def one(x2, w1_ref, w2_ref, pe_ref):
        top = _dot(x2, w1_ref[0])
        sbuf[0:ncp, :] = _dot(x2, w1_ref[1])
        sbuf[ncp:ncp + SUBLANE, :] = jnp.zeros((SUBLANE, sbuf.shape[1]), F32)
        bias = _dot(pe_ref[0], w1_ref[0]) + _dot(pe_ref[1], w1_ref[1])
        hid = top + sbuf[pl.ds(1, ncp), :] + bias[0:1, :]
        return _dot(_gelu(hid).astype(BF16), w2_ref[...])

    kc_ref[0] = one(k2_ref[0], w1k_ref, w2k_ref, pek_ref).astype(BF16)
    vct_ref[0] = one(v2_ref[0], w1v_ref, w2v_ref, pev_ref).T.astype(BF16)


def _compress(k2, v2, w1k, w2k, pek, w1v, w2v, pev):
    batch, ncp, width = k2.shape
    hid2 = A_KV_HEADS * CMP_HIDDEN
    src = pl.BlockSpec((1, ncp, width), lambda b: (b, 0, 0))
    return pl.pallas_call(
        _compress_kernel,
        out_shape=(jax.ShapeDtypeStruct((batch, ncp, LANE), BF16),
                   jax.ShapeDtypeStruct((batch, LANE, ncp), BF16)),
        grid=(batch,),
        in_specs=[src, src,
                  _resident((2, width, hid2)), _resident((hid2, LANE)), _resident((2, SUBLANE, width)),
                  _resident((2, width, hid2)), _resident((hid2, LANE)), _resident((2, SUBLANE, width))],
        out_specs=(pl.BlockSpec((1, ncp, LANE), lambda b: (b, 0, 0)),
                   pl.BlockSpec((1, LANE, ncp), lambda b: (b, 0, 0))),
        scratch_shapes=[pltpu.VMEM((ncp + SUBLANE, hid2), F32)],
        compiler_params=_params(1),
        name="compress",
    )(k2, v2, w1k, w2k, pek, w1v, w2v, pev)


_SLC, _WIN = 0, 1
_ONES_ROWS = 16


def _attn_kernel(q_ref, kc_ref, vct_ref, ks_ref, vst_ref, kw_ref, vwt_ref, gate_ref, blk_ref, o_ref,
                 ps_ref, sel_ref, rank_ref, gt_ref, m_ref, acc_ref, snx_ref, pdf_ref, adf_ref,
                 *, n_sel):
    qb = pl.program_id(1)
    ncp = kc_ref.shape[1]
    per = SLC_BLOCK // CMP_STRIDE
    nb = ncp // per
    nq = A_GROUP * Q_BLOCK
    blocks_per_chunk = KEY_CHUNK // SLC_BLOCK
    t0 = qb * Q_BLOCK
    last = qb // (KEY_CHUNK // Q_BLOCK)
    n_pairs = last // 2
    s0, s1 = _SLC * A_KV_HEADS, _SLC * A_KV_HEADS + 1

    lane = lax.broadcasted_iota(jnp.int32, (Q_BLOCK, LANE), 1)
    qmats = []
    for hk in range(A_KV_HEADS):
        head_mask = jnp.where((lane // 32) % 2 == hk, 1.0, 0.0).astype(BF16)
        qmats.append(jnp.concatenate(
            [q_ref[:, g * LANE:(g + 1) * LANE] * head_mask for g in range(A_GROUP)], axis=0))
    q_t = [qm.T for qm in qmats]
    pos = t0 + lax.broadcasted_iota(jnp.int32, (1, nq), 1) % Q_BLOCK
    chunk_rows = lax.broadcasted_iota(jnp.int32, (KEY_CHUNK, 1), 0)
    ones_rows = jnp.ones((_ONES_ROWS, KEY_CHUNK), BF16)

    def key_chunk(k_ref, c):
        return k_ref[0, pl.ds(pl.multiple_of(c * KEY_CHUNK, KEY_CHUNK), KEY_CHUNK), :]

    def vt_of(vt_ref, c, hk):
        return jnp.concatenate([vt_ref[0, c, hk * HEAD_DIM:(hk + 1) * HEAD_DIM, :], ones_rows], axis=0)

    def softmax_update(slot, s):
        m_old = m_ref[slot]
        m_new = jnp.maximum(m_old, jnp.max(s, axis=0, keepdims=True))
        m_ref[slot] = m_new
        return jnp.exp2(s - m_new).astype(BF16), jnp.exp2(m_old - m_new)

    def pv_update(slot, vt, p, alpha):
        acc_ref[slot] = alpha * acc_ref[slot] + _dot(vt, p)

    def causal(s, c):
        return jnp.where(chunk_rows <= pos - c * KEY_CHUNK, s, -jnp.inf)

    for hk in range(A_KV_HEADS):
        snx_ref[hk] = _dot(key_chunk(ks_ref, 0), q_t[hk])
    m_ref[...] = jnp.full(m_ref.shape, M_FLOOR, F32)
    acc_ref[...] = jnp.zeros(acc_ref.shape, F32)
    pdf_ref[...] = jnp.zeros(pdf_ref.shape, BF16)
    adf_ref[...] = jnp.ones(adf_ref.shape, F32)

    cend = lax.broadcasted_iota(jnp.int32, (ncp, 1), 0) * CMP_STRIDE + (CMP_LEN - 1)
    bidx = lax.broadcasted_iota(jnp.int32, (nb, Q_BLOCK), 0)
    cur = (t0 + lax.broadcasted_iota(jnp.int32, (1, Q_BLOCK), 1)) // SLC_BLOCK
    forced = (bidx == 0) | (bidx == cur) | (bidx == cur - 1)
    o_cmp = []
    sc_all = [jnp.where(cend <= pos, _dot(kc_ref[0], q_t[hk]), -jnp.inf)
              for hk in range(A_KV_HEADS)]
    for hk in range(A_KV_HEADS):
        sc = sc_all[hk]
        m_c = jnp.maximum(jnp.max(sc, axis=0, keepdims=True), M_FLOOR)
        p_c = jnp.exp2(sc - m_c)
        pn = p_c * (1.0 / jnp.maximum(jnp.sum(p_c, axis=0, keepdims=True), 1e-30))
        o_cmp.append(_dot(vct_ref[0, hk * HEAD_DIM:(hk + 1) * HEAD_DIM, :], pn.astype(BF16)))
        psum = pn[:, 0:Q_BLOCK]
        for g in range(1, A_GROUP):
            psum = psum + pn[:, g * Q_BLOCK:(g + 1) * Q_BLOCK]
        ps_ref[hk, 0:SUBLANE, :] = jnp.zeros((SUBLANE, Q_BLOCK), F32)
        ps_ref[hk, SUBLANE:SUBLANE + ncp, :] = psum
        imp = ps_ref[hk, pl.ds(SUBLANE - 1, nb, stride=per), :]
        for k in range(per):
            imp = imp + ps_ref[hk, pl.ds(SUBLANE + k, nb, stride=per), :]
        sel_ref[hk, 0:nb, :] = jnp.where(forced, 1e4, jnp.where(bidx <= cur, imp, -1e4))

    rank_ref[...] = jnp.zeros(rank_ref.shape, F32)
    rows8 = lax.broadcasted_iota(jnp.int32, (SUBLANE, Q_BLOCK), 0)
    n_slab = nb // SUBLANE
    for grp in range(n_slab):
        @pl.when(grp * SUBLANE <= t0 // SLC_BLOCK + 1)
        def _():
            for hk in range(A_KV_HEADS):
                slabs = [sel_ref[hk, r * SUBLANE:(r + 1) * SUBLANE, :] for r in range(n_slab)]
                ranks = [rank_ref[hk, r * SUBLANE:(r + 1) * SUBLANE, :] for r in range(n_slab)]
                for k in range(grp * SUBLANE, (grp + 1) * SUBLANE):
                    rk = sel_ref[hk, k:k + 1, :]
                    for r in range(n_slab):
                        if r * SUBLANE > k:
                            beats = rk >= slabs[r]
                        elif (r + 1) * SUBLANE <= k:
                            beats = rk > slabs[r]
                        else:
                            beats = (rk > slabs[r]) | ((rk == slabs[r]) & (rows8 + r * SUBLANE > k))
                        ranks[r] = ranks[r] + jnp.where(beats, 1.0, 0.0)
                for r in range(n_slab):
                    rank_ref[hk, r * SUBLANE:(r + 1) * SUBLANE, :] = ranks[r]

    q_aug = []
    for hk in range(A_KV_HEADS):
        bias = jnp.where(rank_ref[hk] < n_sel, 0.0, M_FLOOR)
        sel_ref[hk, 0:nb, :] = bias
        if nb < LANE:
            sel_ref[hk, nb:LANE, :] = jnp.zeros((LANE - nb, Q_BLOCK), F32)
        bias_rows = jnp.concatenate([sel_ref[hk].astype(BF16)] * A_GROUP, axis=1)
        q_aug.append(jnp.concatenate([q_t[hk], bias_rows], axis=0))

    def slc_scores(c, hk):
        k_aug = jnp.concatenate([key_chunk(ks_ref, c), key_chunk(blk_ref, c)], axis=1)
        return _dot(k_aug, q_aug[hk])

    for hk in range(A_KV_HEADS):
        for bl in range(blocks_per_chunk):
            rows = slice(bl * SLC_BLOCK, (bl + 1) * SLC_BLOCK)
            row = jnp.concatenate([sel_ref[hk, bl:bl + 1, :]] * A_GROUP, axis=1)
            snx_ref[hk, rows, :] = snx_ref[hk, rows, :] + row

    def pair_body(i, carry):
        ca = 2 * i
        cb = ca + 1
        pv_update(s1, vt_of(vst_ref, jnp.maximum(ca - 1, 0), 1), pdf_ref[...], adf_ref[...])
        sc_ = slc_scores(cb, 0)
        sd = slc_scores(cb, 1)
        pa, aa = softmax_update(s0, snx_ref[0])
        pv_update(s0, vt_of(vst_ref, ca, 0), pa, aa)
        pb, ab = softmax_update(s1, snx_ref[1])
        snx_ref[0] = slc_scores(cb + 1, 0)
        pv_update(s1, vt_of(vst_ref, ca, 1), pb, ab)
        snx_ref[1] = slc_scores(cb + 1, 1)
        pc, ac = softmax_update(s0, sc_)
        pv_update(s0, vt_of(vst_ref, cb, 0), pc, ac)
        pd, ad = softmax_update(s1, sd)
        pdf_ref[...] = pd
        adf_ref[...] = ad
        return carry

    lax.fori_loop(0, n_pairs, pair_body, 0)

    c0 = 2 * n_pairs
    pv_update(s1, vt_of(vst_ref, jnp.maximum(c0 - 1, 0), 1), pdf_ref[...], adf_ref[...])

    def slc_tail(c, hk, prefetched):
        def score():
            return causal(snx_ref[hk] if prefetched else slc_scores(c, hk), c)
        return (_SLC * A_KV_HEADS + hk, score, lambda: vt_of(vst_ref, c, hk))

    def win_tail(c, hk, diag):
        cc = jnp.maximum(c, 0)

        def score():
            s = _dot(key_chunk(kw_ref, cc), q_t[hk])
            if diag:
                return causal(s, c)
            low = jnp.where(c >= 0, pos - WINDOW - c * KEY_CHUNK, KEY_CHUNK)
            return jnp.where(chunk_rows > low, s, -jnp.inf)
        return (_WIN * A_KV_HEADS + hk, score, lambda: vt_of(vwt_ref, cc, hk))

    units = [slc_tail(c0, 0, True), slc_tail(c0, 1, True),
             win_tail(last - 2, 0, False), win_tail(last - 2, 1, False),
             slc_tail(c0 + 1, 0, False), slc_tail(c0 + 1, 1, False),
             win_tail(last - 1, 0, False), win_tail(last - 1, 1, False),
             win_tail(last, 0, True), win_tail(last, 1, True)]
    ahead = 2
    scores = [u[1]() for u in units[:ahead]]
    for i, (slot, _, vt_fn) in enumerate(units):
        if i + ahead < len(units):
            scores.append(units[i + ahead][1]())
        p, alpha = softmax_update(slot, scores[i])
        pv_update(slot, vt_fn(), p, alpha)

    gt_ref[...] = gate_ref[...].T
    for hk in range(A_KV_HEADS):
        def gate(br):
            base = br * A_HEADS + hk * A_GROUP
            return jnp.concatenate([gt_ref[base + g:base + g + 1, :] for g in range(A_GROUP)], axis=1)

        def branch(slot):
            acc = acc_ref[slot]
            return acc[0:HEAD_DIM, :] * (1.0 / jnp.maximum(acc[HEAD_DIM:HEAD_DIM + 1, :], 1e-30))

        o = (gate(0) * o_cmp[hk] + gate(1) * branch(_SLC * A_KV_HEADS + hk)
             + gate(2) * branch(_WIN * A_KV_HEADS + hk))
        for g in range(A_GROUP):
            r0 = (hk * A_GROUP + g) * HEAD_DIM
            o_ref[0, r0:r0 + HEAD_DIM, :] = o[:, g * Q_BLOCK:(g + 1) * Q_BLOCK].astype(BF16)


def _attention(q, kc, vct, ks, vst, kw, vwt, gates, batch, seq):
    nqb = seq // Q_BLOCK
    ncp = kc.shape[1]
    nch = seq // KEY_CHUNK
    nb = seq // SLC_BLOCK
    assert nb <= LANE
    nq = A_GROUP * Q_BLOCK
    n_state = 2 * A_KV_HEADS
    ks3 = ks.reshape(batch, seq, LANE)
    kw3 = kw.reshape(batch, seq, LANE)
    block_onehot = (jnp.arange(seq)[:, None] // SLC_BLOCK == jnp.arange(LANE)[None, :]).astype(BF16)
    kspec = pl.BlockSpec((1, seq, LANE), lambda b, t: (b, 0, 0))
    vspec = pl.BlockSpec((1, nch, LANE, KEY_CHUNK), lambda b, t: (b, 0, 0, 0))
    return pl.pallas_call(
        functools.partial(_attn_kernel, n_sel=min(SLC_TOPK, nb)),
        out_shape=jax.ShapeDtypeStruct((batch, A_WIDTH, seq), BF16),
        grid=(batch, nqb),
        in_specs=[pl.BlockSpec((Q_BLOCK, A_WIDTH), lambda b, t: (b * nqb + t, 0)),
                  pl.BlockSpec((1, ncp, LANE), lambda b, t: (b, 0, 0)),
                  pl.BlockSpec((1, LANE, ncp), lambda b, t: (b, 0, 0)),
                  kspec, vspec, kspec, vspec,
                  pl.BlockSpec((Q_BLOCK, LANE), lambda b, t: (b * nqb + t, 0)),
                  _resident((1, seq, LANE))],
        out_specs=pl.BlockSpec((1, A_WIDTH, Q_BLOCK), lambda b, t: (b, 0, t)),
        scratch_shapes=[pltpu.VMEM((A_KV_HEADS, ncp + SUBLANE, Q_BLOCK), F32),
                        pltpu.VMEM((A_KV_HEADS, LANE, Q_BLOCK), F32),
                        pltpu.VMEM((A_KV_HEADS, nb, Q_BLOCK), F32),
                        pltpu.VMEM((LANE, Q_BLOCK), F32),
                        pltpu.VMEM((n_state, 1, nq), F32),
                        pltpu.VMEM((n_state, HEAD_DIM + _ONES_ROWS, nq), F32),
                        pltpu.VMEM((A_KV_HEADS, KEY_CHUNK, nq), F32),
                        pltpu.VMEM((KEY_CHUNK, nq), BF16),
                        pltpu.VMEM((1, nq), F32)],
        compiler_params=_params(2),
        name="nsa_attention",
    )(q, kc, vct, ks3, vst, kw3, vwt, gates, block_onehot.reshape(1, seq, LANE))


def _rglru_kernel(x_ref, g_ref, win_ref, bin_ref, cw_ref, cb_ref, wai_ref, ba_ref, bi_ref, lam_ref,
                  wout_ref, bout_ref, o_ref, zbuf, abuf, bbuf, hbuf, hprev, *, tiles_per_batch):
    tm = x_ref.shape[0]
    i = pl.program_id(0)

    @pl.when(i % tiles_per_batch == 0)
    def _():
        zbuf[0:SUBLANE, :] = jnp.zeros((SUBLANE, R_WIDTH), F32)
        hprev[...] = jnp.zeros((SUBLANE, R_WIDTH), F32)

    x = x_ref[...]
    hb = _rms(x, g_ref[...]).astype(BF16)
    y = _gelu(_dot(hb, win_ref[:, 0:R_WIDTH]) + bin_ref[:, 0:R_WIDTH])
    z2 = _dot(hb, win_ref[:, R_WIDTH:2 * R_WIDTH]) + bin_ref[:, R_WIDTH:2 * R_WIDTH]
    zbuf[SUBLANE:SUBLANE + tm, :] = z2
    cw = cw_ref[...]
    u = (cw[0:1, :] * zbuf[pl.ds(SUBLANE - 3, tm), :]
         + cw[1:2, :] * zbuf[pl.ds(SUBLANE - 2, tm), :]
         + cw[2:3, :] * zbuf[pl.ds(SUBLANE - 1, tm), :]
         + cw[3:4, :] * z2 + cb_ref[...])
    zbuf[0:SUBLANE, :] = z2[tm - SUBLANE:tm, :]

    ub = u.astype(BF16)
    log_sig = jax.nn.log_sigmoid(lam_ref[...])
    for n in range(R_BLOCKS):
        cols = slice(n * R_BLOCK_DIM, (n + 1) * R_BLOCK_DIM)
        ri = _dot(ub[:, cols], wai_ref[n])
        r = jax.nn.sigmoid(ri[:, 0:R_BLOCK_DIM] + ba_ref[:, cols])
        gi = jax.nn.sigmoid(ri[:, R_BLOCK_DIM:2 * R_BLOCK_DIM] + bi_ref[:, cols])
        log_a = (LRU_C * log_sig[:, cols]) * r
        a = jnp.exp(log_a)
        mult = jnp.sqrt(-jnp.tanh(log_a) * (a * a + 1.0))
        abuf[:, cols] = a
        bbuf[:, cols] = mult * (gi * u[:, cols])

    rows = lax.broadcasted_iota(jnp.int32, (SUBLANE, R_WIDTH), 0)

    def slab(j, hp):
        r0 = pl.multiple_of(j * SUBLANE, SUBLANE)
        a = abuf[pl.ds(r0, SUBLANE), :]
        b = bbuf[pl.ds(r0, SUBLANE), :]
        for d in (1, 2, 4):
            keep = rows >= d
            b = jnp.where(keep, a * pltpu.roll(b, d, 0) + b, b)
            a = jnp.where(keep, a * pltpu.roll(a, d, 0), a)
        h = a * hp + b
        hbuf[pl.ds(r0, SUBLANE), :] = h
        return jnp.broadcast_to(h[SUBLANE - 1:SUBLANE, :], (SUBLANE, R_WIDTH))

    hprev[...] = lax.fori_loop(0, tm // SUBLANE, slab, hprev[...])
    o_ref[...] = x + _dot((hbuf[...] * y).astype(BF16), wout_ref[...]) + bout_ref[...]


def _rglru(x2, g, win, b_in, conv_w, conv_b, wai, b_a, b_i, lam, wout, b_out, seq):
    n = x2.shape[0]
    tm = ROW_TILE
    tpb = seq // tm
    row = pl.BlockSpec((tm, D_MODEL), lambda i: (i, 0))
    buf = lambda r: pltpu.VMEM((r, R_WIDTH), F32)
    return pl.pallas_call(
        functools.partial(_rglru_kernel, tiles_per_batch=tpb),
        out_shape=jax.ShapeDtypeStruct((n, D_MODEL), F32),
        grid=(n // tm,),
        in_specs=[row, _resident((1, D_MODEL)), _resident((D_MODEL, 2 * R_WIDTH)), _resident((1, 2 * R_WIDTH)),
                  _resident((R_CONV, R_WIDTH)), _resident((1, R_WIDTH)),
                  _resident((R_BLOCKS, R_BLOCK_DIM, 2 * R_BLOCK_DIM)),
                  _resident((1, R_WIDTH)), _resident((1, R_WIDTH)), _resident((1, R_WIDTH)),
                  _resident((R_WIDTH, D_MODEL)), _resident((1, D_MODEL))],
        out_specs=row,
        scratch_shapes=[buf(tm + SUBLANE), buf(tm), buf(tm), buf(tm), buf(SUBLANE)],
        compiler_params=_params(1),
        name="rglru_mixer",
    )(x2, g, win, b_in, conv_w, conv_b, wai, b_a, b_i, lam, wout, b_out)


def _mix_ffn_kernel(x_ref, ot_ref, ob_ref, wo_ref, *rest, **static):
    mixed = jnp.concatenate([ot_ref[0].T, ob_ref[...]], axis=1)
    _ffn_body(x_ref[...] + _dot(mixed, wo_ref[...]), *rest, **static)


def _ffn_kernel(x_ref, *rest, **static):
    _ffn_body(x_ref[...], *rest, **static)


def _ffn_body(x, g_ref, wup_ref, cw_ref, cb_ref, wdn_ref, gf_ref, o_ref, ubuf, carry, acc,
              *, tiles_per_batch, final_norm):
    tm = x.shape[0]
    i = pl.program_id(0)

    @pl.when(i % tiles_per_batch == 0)
    def _():
        carry[...] = jnp.zeros(carry.shape, F32)

    hb = _rms(x, g_ref[...]).astype(BF16)
    acc[...] = x
    chunks = [(c0, min(FF_CHUNK, D_FF - c0)) for c0 in range(0, D_FF, FF_CHUNK)]

    def up(ci):
        c0, width = chunks[ci]
        return [_dot(hb, wup_ref[:, half * D_FF + c0:half * D_FF + c0 + width]) for half in range(2)]

    u_next = up(0)
    for ci, (c0, width) in enumerate(chunks):
        u_cur = u_next
        if ci + 1 < len(chunks):
            u_next = up(ci + 1)
        branch = []
        for half in range(2):
            col = half * D_FF + c0
            cols = slice(col, col + width)
            u = u_cur[half]
            ubuf[half, 0:SUBLANE, 0:width] = carry[:, cols]
            ubuf[half, SUBLANE:SUBLANE + tm, 0:width] = u
            carry[:, cols] = u[tm - SUBLANE:tm, :]
            w = cw_ref[:, cols]
            branch.append(w[0:1, :] * ubuf[half, pl.ds(SUBLANE - 2, tm), 0:width]
                          + w[1:2, :] * ubuf[half, pl.ds(SUBLANE - 1, tm), 0:width]
                          + w[2:3, :] * u + cb_ref[:, cols])
        act = (branch[0] * jax.nn.sigmoid(branch[0]) * branch[1]).astype(BF16)
        acc[...] += _dot(act, wdn_ref[c0:c0 + width, :])
    out = acc[...]
    if final_norm:
        out = _rms(out, gf_ref[...])
    o_ref[...] = out


def _ffn(x2, mixer, g, wup, conv_w, conv_b, wdn, g_final, seq, final_norm):
    n = x2.shape[0]
    tm = ROW_TILE
    tpb = seq // tm
    row = lambda w: pl.BlockSpec((tm, w), lambda i: (i, 0))
    ffn_specs = [_resident((1, D_MODEL)), _resident((D_MODEL, 2 * D_FF)), _resident((3, 2 * D_FF)),
                 _resident((1, 2 * D_FF)), _resident((D_FF, D_MODEL)), _resident((1, D_MODEL))]
    ffn_args = (g, wup, conv_w, conv_b, wdn, g_final)
    if mixer is None:
        body, specs, args = _ffn_kernel, [row(D_MODEL)], (x2,)
    else:
        body = _mix_ffn_kernel
        specs = [row(D_MODEL), pl.BlockSpec((1, A_WIDTH, tm), lambda i: (i // tpb, 0, i % tpb)),
                 row(B_WIDTH), _resident((A_WIDTH + B_WIDTH, D_MODEL))]
        args = (x2,) + tuple(mixer)
    return pl.pallas_call(
        functools.partial(body, tiles_per_batch=tpb, final_norm=final_norm),
        out_shape=jax.ShapeDtypeStruct((n, D_MODEL), F32),
        grid=(n // tm,),
        in_specs=specs + ffn_specs,
        out_specs=row(D_MODEL),
        scratch_shapes=[pltpu.VMEM((2, tm + SUBLANE, FF_CHUNK), F32),
                        pltpu.VMEM((SUBLANE, 2 * D_FF), F32),
                        pltpu.VMEM((tm, D_MODEL), F32)],
        compiler_params=_params(1),
        name="conv_ffn",
    )(*args, *ffn_args)


def _rope_tables(seq):
    inv = 1.0 / (ROPE_THETA ** (jnp.arange(0, HEAD_DIM, 2, dtype=F32) / HEAD_DIM))
    ang = jnp.arange(seq, dtype=F32)[:, None] * inv[None, :]
    ang = jnp.tile(ang, (1, LANE // (HEAD_DIM // 2)))
    sign = jnp.where(jnp.arange(LANE) < LANE // 2, -1.0, 1.0).astype(F32)
    return jnp.cos(ang), jnp.sin(ang) * sign[None, :]


def _even_in_weights(w_in):
    kv0 = A_WIDTH
    part = lambda p: w_in[:, kv0 + p * KV_COLS: kv0 + (p + 1) * KV_COLS]
    g0 = kv0 + 6 * KV_COLS
    bc0 = g0 + A_HEADS * N_BRANCH
    scale = (HEAD_DIM ** -0.5) * LOG2E
    gates = jnp.pad(w_in[:, g0:bc0][:, _G_PERM], ((0, 0), (0, LANE - A_HEADS * N_BRANCH)))
    return jnp.concatenate([
        w_in[:, :A_WIDTH][:, _Q_PERM] * scale,
        part(0)[:, _K_PERM], part(2)[:, _K_PERM], part(4)[:, _K_PERM],
        part(1), part(3), part(5),
        gates,
        w_in[:, bc0:],
    ], axis=1).astype(BF16)


def _compress_weights(pe, w1, w2, lane_head, lane_dim):
    onehot = jnp.asarray(lane_head[:, None] == np.arange(A_KV_HEADS)[None, :], F32)
    w1g = w1.reshape(CMP_LEN, HEAD_DIM, CMP_HIDDEN)[:, lane_dim, :]
    w1x = (w1g[:, :, None, :] * onehot[None, :, :, None]).reshape(CMP_LEN * LANE, A_KV_HEADS * CMP_HIDDEN)
    half = (CMP_LEN // 2) * LANE
    w1x = jnp.stack([w1x[:half], w1x[half:]]).astype(BF16)
    w2x = (onehot.T[:, None, :] * w2[:, lane_dim][None, :, :]).reshape(A_KV_HEADS * CMP_HIDDEN, LANE)
    pel = pe[:, lane_dim].reshape(2, 1, half)
    pex = jnp.broadcast_to(pel, (2, SUBLANE, half)).astype(BF16)
    return w1x, w2x.astype(BF16), pex


def kernel(x, norm_mix, norm_ffn, norm_final, a_w_in, a_cmp_pe, a_cmp_w1, a_cmp_w2, a_conv_w, a_w_out,
           c_w_in, c_b_in, c_conv_w, c_conv_b, c_w_a, c_b_a, c_w_i, c_b_i, c_lambda, c_w_out, c_b_out,
           f_w_up, f_conv_w, f_conv_b, f_w_down):
    batch, seq, _ = x.shape
    assert seq % ROW_TILE == 0 and ROW_TILE % KEY_CHUNK == 0 and seq // SLC_BLOCK >= 1
    x2 = x.reshape(batch * seq, D_MODEL)
    cos, sin = _rope_tables(seq)
    row = lambda v: v.reshape(1, -1)
    for layer in range(DEPTH):
        j = layer // 2
        g_mix = row(norm_mix[layer])
        if layer % 2 == 0:
            wcat = _even_in_weights(a_w_in[j])
            q, kcs, ks, kw, vcs, vst, vwt, gates, o_b = _even_in_proj(
                x2, g_mix, wcat, cos, sin, a_conv_w[j], batch, seq)
            ncp = seq // CMP_STRIDE
            w1k, w2k, pek = _compress_weights(a_cmp_pe[j, 0], a_cmp_w1[j, 0], a_cmp_w2[j, 0], _IL_HEAD, _IL_DIM)
            w1v, w2v, pev = _compress_weights(a_cmp_pe[j, 1], a_cmp_w1[j, 1], a_cmp_w2[j, 1], _ST_HEAD, _ST_DIM)
            kc, vct = _compress(kcs.reshape(batch, ncp, CMP_STRIDE * LANE),
                                vcs.reshape(batch, ncp, CMP_STRIDE * LANE),
                                w1k, w2k, pek, w1v, w2v, pev)
            o_t = _attention(q, kc, vct, ks, vst, kw, vwt, gates, batch, seq)
            mixer = (o_t, o_b, a_w_out[j].astype(BF16))
        else:
            mixer = None
            wai = jnp.concatenate([c_w_a[j], c_w_i[j]], axis=-1).astype(BF16)
            x2 = _rglru(x2, g_mix, c_w_in[j].astype(BF16), row(c_b_in[j]), c_conv_w[j], row(c_conv_b[j]),
                        wai, row(c_b_a[j]), row(c_b_i[j]), row(c_lambda[j]),
                        c_w_out[j].astype(BF16), row(c_b_out[j]), seq)
        x2 = _ffn(x2, mixer, row(norm_ffn[layer]), f_w_up[layer].astype(BF16), f_conv_w[layer],
                  row(f_conv_b[layer]), f_w_down[layer].astype(BF16), row(norm_final), seq,
                  final_norm=(layer == DEPTH - 1))
    return x2.reshape(batch, seq, D_MODEL)
```

```python
import functools
import math

import numpy as np
import jax
import jax.numpy as jnp
from jax import lax
from jax.experimental import pallas as pl
from jax.experimental.pallas import tpu as pltpu

F32 = jnp.float32
BF16 = jnp.bfloat16

D_MODEL = 1024
DEPTH = 4
A_HEADS = 8
A_KV_HEADS = 2
A_GROUP = A_HEADS // A_KV_HEADS
HEAD_DIM = 64
CMP_LEN = 32
CMP_STRIDE = 16
CMP_HIDDEN = 128
SLC_BLOCK = 64
SLC_TOPK = 8
WINDOW = 512
Q_BLOCK = 128
N_BRANCH = 3
ROPE_THETA = 10000.0
A_WIDTH = A_HEADS * HEAD_DIM
KV_COLS = A_KV_HEADS * HEAD_DIM
B_WIDTH = D_MODEL // 2
R_WIDTH = D_MODEL
R_BLOCKS = 8
R_BLOCK_DIM = R_WIDTH // R_BLOCKS
R_CONV = 4
LRU_C = 8.0
D_FF = 2816
NORM_EPS = 1e-6

LANE = 128
SUBLANE = 8
V7X_VMEM_BYTES = 64 * 1024 * 1024
VMEM_LIMIT = (V7X_VMEM_BYTES * 7) // 8

ROW_TILE = 512
KEY_CHUNK = 256
FF_CHUNK = 512
M_FLOOR = -1e30
LOG2E = 1.4426950408889634

_L = np.arange(LANE)
_IL_HEAD = (_L // 32) % 2
_IL_DIM = (_L % 32) + 32 * (_L // 64)
_ST_HEAD = _L // HEAD_DIM
_ST_DIM = _L % HEAD_DIM
_K_PERM = _IL_HEAD * HEAD_DIM + _IL_DIM
_Q_PERM = np.concatenate([(_IL_HEAD * A_GROUP + g) * HEAD_DIM + _IL_DIM for g in range(A_GROUP)])
_G_PERM = np.array([hk * A_GROUP * N_BRANCH + g * N_BRANCH + br
                    for br in range(N_BRANCH) for hk in range(A_KV_HEADS) for g in range(A_GROUP)])

_C_Q = 0
_C_KCS = 512
_C_KS = 640
_C_KW = 768
_C_VCS = 896
_C_VS = 1024
_C_VW = 1152
_C_GATE = 1280
_C_BG = 1408
_C_CG = 1920
_C_XG = 2432
_C_END = 2944


def _resident(shape):
    nd = len(shape)
    return pl.BlockSpec(shape, lambda *_: (0,) * nd, pipeline_mode=pl.Buffered(1))


def _params(n_axes):
    return pltpu.CompilerParams(dimension_semantics=("arbitrary",) * n_axes,
                                vmem_limit_bytes=VMEM_LIMIT)


def _rms(x, g):
    return (x * lax.rsqrt(jnp.mean(x * x, axis=-1, keepdims=True) + NORM_EPS)) * g


def _gelu(x):
    return 0.5 * x * (1.0 + jnp.tanh(0.7978845608028654 * (x + 0.044715 * (x * x * x))))


def _sigmoid(x):
    return 0.5 * jnp.tanh(0.5 * x) + 0.5


def _dot(a, b):
    return jnp.dot(a, b, preferred_element_type=F32)


def _even_in_kernel(x_ref, g_ref, w_ref, cos_ref, sin_ref, cw_ref,
                    q_ref, kcs_ref, ks_ref, kw_ref, vcs_ref, vst_ref, vwt_ref, gate_ref, ob_ref,
                    cbuf, *, tiles_per_batch):
    tm = x_ref.shape[0]
    i = pl.program_id(0)
    hb = _rms(x_ref[...], g_ref[...]).astype(BF16)
    cos = cos_ref[...]
    sin = sin_ref[...]

    def rope(z):
        return z * cos + pltpu.roll(z, LANE // 2, 1) * sin

    zq = _dot(hb, w_ref[:, _C_Q:_C_KCS])
    for g in range(A_GROUP):
        q_ref[:, g * LANE:(g + 1) * LANE] = rope(zq[:, g * LANE:(g + 1) * LANE]).astype(BF16)

    zk = _dot(hb, w_ref[:, _C_KCS:_C_VS])
    kcs_ref[...] = rope(zk[:, 0:LANE]).astype(BF16)
    ks_ref[...] = rope(zk[:, LANE:2 * LANE]).astype(BF16)
    kw_ref[...] = rope(zk[:, 2 * LANE:3 * LANE]).astype(BF16)
    vcs_ref[...] = zk[:, 3 * LANE:4 * LANE].astype(BF16)

    zv = _dot(hb, w_ref[:, _C_VS:_C_BG])
    for c in range(tm // KEY_CHUNK):
        rows = slice(c * KEY_CHUNK, (c + 1) * KEY_CHUNK)
        vst_ref[0, c] = zv[rows, 0:LANE].T.astype(BF16)
        vwt_ref[0, c] = zv[rows, LANE:2 * LANE].T.astype(BF16)
    gate_ref[...] = jax.nn.sigmoid(zv[:, 2 * LANE:3 * LANE])

    @pl.when(i % tiles_per_batch == 0)
    def _():
        cbuf[0:SUBLANE, :] = jnp.zeros((SUBLANE, B_WIDTH), F32)

    c0 = _dot(hb, w_ref[:, _C_CG:_C_XG]) * _dot(hb, w_ref[:, _C_XG:_C_END])
    cbuf[SUBLANE:SUBLANE + tm, :] = c0
    cw = cw_ref[...]
    y = (cw[0:1, :] * cbuf[pl.ds(SUBLANE - 2, tm), :]
         + cw[1:2, :] * cbuf[pl.ds(SUBLANE - 1, tm), :]
         + cw[2:3, :] * c0)
    cbuf[0:SUBLANE, :] = c0[tm - SUBLANE:tm, :]
    ob_ref[...] = (_dot(hb, w_ref[:, _C_BG:_C_CG]) * y).astype(BF16)


def _even_in_proj(x2, g, wcat, cos, sin, conv_w, batch, seq):
    n = x2.shape[0]
    tm = ROW_TILE
    tpb = seq // tm
    nch = seq // KEY_CHUNK
    row = lambda w: pl.BlockSpec((tm, w), lambda i: (i, 0))
    vt_spec = pl.BlockSpec((1, tm // KEY_CHUNK, LANE, KEY_CHUNK), lambda i: (i // tpb, i % tpb, 0, 0))
    tab_spec = pl.BlockSpec((tm, LANE), lambda i: (i % tpb, 0))
    out_shape = (
        jax.ShapeDtypeStruct((n, A_WIDTH), BF16),
        jax.ShapeDtypeStruct((n, LANE), BF16),
        jax.ShapeDtypeStruct((n, LANE), BF16),
        jax.ShapeDtypeStruct((n, LANE), BF16),
        jax.ShapeDtypeStruct((n, LANE), BF16),
        jax.ShapeDtypeStruct((batch, nch, LANE, KEY_CHUNK), BF16),
        jax.ShapeDtypeStruct((batch, nch, LANE, KEY_CHUNK), BF16),
        jax.ShapeDtypeStruct((n, LANE), F32),
        jax.ShapeDtypeStruct((n, B_WIDTH), BF16),
    )
    return pl.pallas_call(
        functools.partial(_even_in_kernel, tiles_per_batch=tpb),
        out_shape=out_shape,
        grid=(n // tm,),
        in_specs=[row(D_MODEL), _resident((1, D_MODEL)), _resident((D_MODEL, _C_END)),
                  tab_spec, tab_spec, _resident((3, B_WIDTH))],
        out_specs=(row(A_WIDTH), row(LANE), row(LANE), row(LANE), row(LANE),
                   vt_spec, vt_spec, row(LANE), row(B_WIDTH)),
        scratch_shapes=[pltpu.VMEM((tm + SUBLANE, B_WIDTH), F32)],
        compiler_params=_params(1),
        name="even_in_proj",
    )(x2, g, wcat, cos, sin, conv_w)


def _compress_kernel(k2_ref, v2_ref, w1k_ref, w2k_ref, pek_ref, w1v_ref, w2v_ref, pev_ref,
                     kc_ref, vct_ref, sbuf):
    ncp = k2_ref.shape[1]

    def one(x2, w1_ref, w2_ref, pe_ref):
        top = _dot(x2, w1_ref[0])
        sbuf[0:ncp, :] = _dot(x2, w1_ref[1])
        sbuf[ncp:ncp + SUBLANE, :] = jnp.zeros((SUBLANE, sbuf.shape[1]), F32)
        bias = _dot(pe_ref[0], w1_ref[0]) + _dot(pe_ref[1], w1_ref[1])
        hid = top + sbuf[pl.ds(1, ncp), :] + bias[0:1, :]
        return _dot(_gelu(hid).astype(BF16), w2_ref[...])

    kc_ref[0] = one(k2_ref[0], w1k_ref, w2k_ref, pek_ref).astype(BF16)
    vct_ref[0] = one(v2_ref[0], w1v_ref, w2v_ref, pev_ref).T.astype(BF16)


def _compress(k2, v2, w1k, w2k, pek, w1v, w2v, pev):
    batch, ncp, width = k2.shape
    hid2 = A_KV_HEADS * CMP_HIDDEN
    src = pl.BlockSpec((1, ncp, width), lambda b: (b, 0, 0))
    return pl.pallas_call(
        _compress_kernel,
        out_shape=(jax.ShapeDtypeStruct((batch, ncp, LANE), BF16),
                   jax.ShapeDtypeStruct((batch, LANE, ncp), BF16)),
        grid=(batch,),
        in_specs=[src, src,
                  _resident((2, width, hid2)), _resident((hid2, LANE)), _resident((2, SUBLANE, width)),
                  _resident((2, width, hid2)), _resident((hid2, LANE)), _resident((2, SUBLANE, width))],
        out_specs=(pl.BlockSpec((1, ncp, LANE), lambda b: (b, 0, 0)),
                   pl.BlockSpec((1, LANE, ncp), lambda b: (b, 0, 0))),
        scratch_shapes=[pltpu.VMEM((ncp + SUBLANE, hid2), F32)],
        compiler_params=_params(1),
        name="compress",
    )(k2, v2, w1k, w2k, pek, w1v, w2v, pev)


_SLC, _WIN = 0, 1
_ONES_ROWS = 16


def _attn_kernel(q_ref, kc_ref, vct_ref, ks_ref, vst_ref, kw_ref, vwt_ref, gate_ref, blk_ref, o_ref,
                 ps_ref, sel_ref, rank_ref, gt_ref, m_ref, acc_ref, snx_ref, mnx_ref, pdf_ref, adf_ref,
                 *, n_sel):
    qb = pl.program_id(1)
    ncp = kc_ref.shape[1]
    per = SLC_BLOCK // CMP_STRIDE
    nb = ncp // per
    nq = A_GROUP * Q_BLOCK
    blocks_per_chunk = KEY_CHUNK // SLC_BLOCK
    t0 = qb * Q_BLOCK
    last = qb // (KEY_CHUNK // Q_BLOCK)
    n_pairs = last // 2
    s0, s1 = _SLC * A_KV_HEADS, _SLC * A_KV_HEADS + 1

    lane = lax.broadcasted_iota(jnp.int32, (Q_BLOCK, LANE), 1)
    qmats = []
    for hk in range(A_KV_HEADS):
        head_mask = jnp.where((lane // 32) % 2 == hk, 1.0, 0.0).astype(BF16)
        qmats.append(jnp.concatenate(
            [q_ref[:, g * LANE:(g + 1) * LANE] * head_mask for g in range(A_GROUP)], axis=0))
    q_t = [qm.T for qm in qmats]
    pos = t0 + lax.broadcasted_iota(jnp.int32, (1, nq), 1) % Q_BLOCK
    chunk_rows = lax.broadcasted_iota(jnp.int32, (KEY_CHUNK, 1), 0)
    ones_rows = jnp.ones((_ONES_ROWS, KEY_CHUNK), BF16)

    def key_chunk(k_ref, c):
        return k_ref[0, pl.ds(pl.multiple_of(c * KEY_CHUNK, KEY_CHUNK), KEY_CHUNK), :]

    def vt_of(vt_ref, c, hk):
        return jnp.concatenate([vt_ref[0, c, hk * HEAD_DIM:(hk + 1) * HEAD_DIM, :], ones_rows], axis=0)

    def with_max(s):
        return s, jnp.max(s, axis=0, keepdims=True)

    def softmax_update(slot, scored):
        s, s_max = scored
        m_old = m_ref[slot]
        m_new = jnp.maximum(m_old, s_max)
        m_ref[slot] = m_new
        return jnp.exp2(s - m_new).astype(BF16), jnp.exp2(m_old - m_new)

    def pv_update(slot, vt, p, alpha):
        acc_ref[slot] = alpha * acc_ref[slot] + _dot(vt, p)

    def causal(s, c):
        return jnp.where(chunk_rows <= pos - c * KEY_CHUNK, s, -jnp.inf)

    for hk in range(A_KV_HEADS):
        snx_ref[hk] = _dot(key_chunk(ks_ref, 0), q_t[hk])
    m_ref[...] = jnp.full(m_ref.shape, M_FLOOR, F32)
    acc_ref[...] = jnp.zeros(acc_ref.shape, F32)
    pdf_ref[...] = jnp.zeros(pdf_ref.shape, BF16)
    adf_ref[...] = jnp.ones(adf_ref.shape, F32)

    cend = lax.broadcasted_iota(jnp.int32, (ncp, 1), 0) * CMP_STRIDE + (CMP_LEN - 1)
    bidx = lax.broadcasted_iota(jnp.int32, (nb, Q_BLOCK), 0)
    cur = (t0 + lax.broadcasted_iota(jnp.int32, (1, Q_BLOCK), 1)) // SLC_BLOCK
    forced = (bidx == 0) | (bidx == cur) | (bidx == cur - 1)
    o_cmp = []
    sc_all = [jnp.where(cend <= pos, _dot(kc_ref[0], q_t[hk]), -jnp.inf)
              for hk in range(A_KV_HEADS)]
    for hk in range(A_KV_HEADS):
        sc = sc_all[hk]
        m_c = jnp.maximum(jnp.max(sc, axis=0, keepdims=True), M_FLOOR)
        p_c = jnp.exp2(sc - m_c)
        pn = p_c * (1.0 / jnp.maximum(jnp.sum(p_c, axis=0, keepdims=True), 1e-30))
        o_cmp.append(_dot(vct_ref[0, hk * HEAD_DIM:(hk + 1) * HEAD_DIM, :], pn.astype(BF16)))
        psum = pn[:, 0:Q_BLOCK]
        for g in range(1, A_GROUP):
            psum = psum + pn[:, g * Q_BLOCK:(g + 1) * Q_BLOCK]
        ps_ref[hk, 0:SUBLANE, :] = jnp.zeros((SUBLANE, Q_BLOCK), F32)
        ps_ref[hk, SUBLANE:SUBLANE + ncp, :] = psum
        imp = ps_ref[hk, pl.ds(SUBLANE - 1, nb, stride=per), :]
        for k in range(per):
            imp = imp + ps_ref[hk, pl.ds(SUBLANE + k, nb, stride=per), :]
        sel_ref[hk, 0:nb, :] = jnp.where(forced, 1e4, jnp.where(bidx <= cur, imp, -1e4))

    rank_ref[...] = jnp.zeros(rank_ref.shape, F32)
    rows8 = lax.broadcasted_iota(jnp.int32, (SUBLANE, Q_BLOCK), 0)
    n_slab = nb // SUBLANE
    for grp in range(n_slab):
        @pl.when(grp * SUBLANE <= t0 // SLC_BLOCK + 1)
        def _():
            for hk in range(A_KV_HEADS):
                slabs = [sel_ref[hk, r * SUBLANE:(r + 1) * SUBLANE, :] for r in range(n_slab)]
                ranks = [rank_ref[hk, r * SUBLANE:(r + 1) * SUBLANE, :] for r in range(n_slab)]
                for k in range(grp * SUBLANE, (grp + 1) * SUBLANE):
                    rk = sel_ref[hk, k:k + 1, :]
                    for r in range(n_slab):
                        if r * SUBLANE > k:
                            beats = rk >= slabs[r]
                        elif (r + 1) * SUBLANE <= k:
                            beats = rk > slabs[r]
                        else:
                            beats = (rk > slabs[r]) | ((rk == slabs[r]) & (rows8 + r * SUBLANE > k))
                        ranks[r] = ranks[r] + jnp.where(beats, 1.0, 0.0)
                for r in range(n_slab):
                    rank_ref[hk, r * SUBLANE:(r + 1) * SUBLANE, :] = ranks[r]

    q_aug = []
    for hk in range(A_KV_HEADS):
        bias = jnp.where(rank_ref[hk] < n_sel, 0.0, M_FLOOR)
        sel_ref[hk, 0:nb, :] = bias
        if nb < LANE:
            sel_ref[hk, nb:LANE, :] = jnp.zeros((LANE - nb, Q_BLOCK), F32)
        bias_rows = jnp.concatenate([sel_ref[hk].astype(BF16)] * A_GROUP, axis=1)
        q_aug.append(jnp.concatenate([q_t[hk], bias_rows], axis=0))

    def slc_scores(c, hk):
        k_aug = jnp.concatenate([key_chunk(ks_ref, c), key_chunk(blk_ref, c)], axis=1)
        return _dot(k_aug, q_aug[hk])

    for hk in range(A_KV_HEADS):
        for bl in range(blocks_per_chunk):
            rows = slice(bl * SLC_BLOCK, (bl + 1) * SLC_BLOCK)
            row = jnp.concatenate([sel_ref[hk, bl:bl + 1, :]] * A_GROUP, axis=1)
            snx_ref[hk, rows, :] = snx_ref[hk, rows, :] + row
        mnx_ref[hk] = jnp.max(snx_ref[hk], axis=0, keepdims=True)

    def pair_body(i, carry):
        ca = 2 * i
        cb = ca + 1
        pv_update(s1, vt_of(vst_ref, jnp.maximum(ca - 1, 0), 1), pdf_ref[...], adf_ref[...])
        sc_ = with_max(slc_scores(cb, 0))
        sd = with_max(slc_scores(cb, 1))
        pa, aa = softmax_update(s0, (snx_ref[0], mnx_ref[0]))
        pv_update(s0, vt_of(vst_ref, ca, 0), pa, aa)
        pb, ab = softmax_update(s1, (snx_ref[1], mnx_ref[1]))
        snx_ref[0], mnx_ref[0] = with_max(slc_scores(cb + 1, 0))
        pv_update(s1, vt_of(vst_ref, ca, 1), pb, ab)
        snx_ref[1], mnx_ref[1] = with_max(slc_scores(cb + 1, 1))
        pc, ac = softmax_update(s0, sc_)
        pv_update(s0, vt_of(vst_ref, cb, 0), pc, ac)
        pd, ad = softmax_update(s1, sd)
        pdf_ref[...] = pd
        adf_ref[...] = ad
        return carry

    lax.fori_loop(0, n_pairs, pair_body, 0)

    c0 = 2 * n_pairs
    pv_update(s1, vt_of(vst_ref, jnp.maximum(c0 - 1, 0), 1), pdf_ref[...], adf_ref[...])

    def slc_tail(c, hk, prefetched):
        def score():
            return with_max(causal(snx_ref[hk] if prefetched else slc_scores(c, hk), c))
        return (_SLC * A_KV_HEADS + hk, score, lambda: vt_of(vst_ref, c, hk))

    def win_tail(c, hk, diag):
        cc = jnp.maximum(c, 0)

        def score():
            s = _dot(key_chunk(kw_ref, cc), q_t[hk])
            if diag:
                return with_max(causal(s, c))
            low = jnp.where(c >= 0, pos - WINDOW - c * KEY_CHUNK, KEY_CHUNK)
            return with_max(jnp.where(chunk_rows > low, s, -jnp.inf))
        return (_WIN * A_KV_HEADS + hk, score, lambda: vt_of(vwt_ref, cc, hk))

    units = [slc_tail(c0, 0, True), slc_tail(c0, 1, True),
             win_tail(last - 2, 0, False), win_tail(last - 2, 1, False),
             slc_tail(c0 + 1, 0, False), slc_tail(c0 + 1, 1, False),
             win_tail(last - 1, 0, False), win_tail(last - 1, 1, False),
             win_tail(last, 0, True), win_tail(last, 1, True)]
    ahead = 2
    scores = [u[1]() for u in units[:ahead]]
    for i, (slot, _, vt_fn) in enumerate(units):
        if i + ahead < len(units):
            scores.append(units[i + ahead][1]())
        p, alpha = softmax_update(slot, scores[i])
        pv_update(slot, vt_fn(), p, alpha)

    gt_ref[...] = gate_ref[...].T
    for hk in range(A_KV_HEADS):
        def gate(br):
            base = br * A_HEADS + hk * A_GROUP
            return jnp.concatenate([gt_ref[base + g:base + g + 1, :] for g in range(A_GROUP)], axis=1)

        def branch(slot):
            acc = acc_ref[slot]
            return acc[0:HEAD_DIM, :] * (1.0 / jnp.maximum(acc[HEAD_DIM:HEAD_DIM + 1, :], 1e-30))

        o = (gate(0) * o_cmp[hk] + gate(1) * branch(_SLC * A_KV_HEADS + hk)
             + gate(2) * branch(_WIN * A_KV_HEADS + hk))
        for g in range(A_GROUP):
            r0 = (hk * A_GROUP + g) * HEAD_DIM
            o_ref[0, r0:r0 + HEAD_DIM, :] = o[:, g * Q_BLOCK:(g + 1) * Q_BLOCK].astype(BF16)


def _attention(q, kc, vct, ks, vst, kw, vwt, gates, batch, seq):
    nqb = seq // Q_BLOCK
    ncp = kc.shape[1]
    nch = seq // KEY_CHUNK
    nb = seq // SLC_BLOCK
    assert nb <= LANE
    nq = A_GROUP * Q_BLOCK
    n_state = 2 * A_KV_HEADS
    ks3 = ks.reshape(batch, seq, LANE)
    kw3 = kw.reshape(batch, seq, LANE)
    block_onehot = (jnp.arange(seq)[:, None] // SLC_BLOCK == jnp.arange(LANE)[None, :]).astype(BF16)
    kspec = pl.BlockSpec((1, seq, LANE), lambda b, t: (b, 0, 0))
    vspec = pl.BlockSpec((1, nch, LANE, KEY_CHUNK), lambda b, t: (b, 0, 0, 0))
    return pl.pallas_call(
        functools.partial(_attn_kernel, n_sel=min(SLC_TOPK, nb)),
        out_shape=jax.ShapeDtypeStruct((batch, A_WIDTH, seq), BF16),
        grid=(batch, nqb),
        in_specs=[pl.BlockSpec((Q_BLOCK, A_WIDTH), lambda b, t: (b * nqb + t, 0)),
                  pl.BlockSpec((1, ncp, LANE), lambda b, t: (b, 0, 0)),
                  pl.BlockSpec((1, LANE, ncp), lambda b, t: (b, 0, 0)),
                  kspec, vspec, kspec, vspec,
                  pl.BlockSpec((Q_BLOCK, LANE), lambda b, t: (b * nqb + t, 0)),
                  _resident((1, seq, LANE))],
        out_specs=pl.BlockSpec((1, A_WIDTH, Q_BLOCK), lambda b, t: (b, 0, t)),
        scratch_shapes=[pltpu.VMEM((A_KV_HEADS, ncp + SUBLANE, Q_BLOCK), F32),
                        pltpu.VMEM((A_KV_HEADS, LANE, Q_BLOCK), F32),
                        pltpu.VMEM((A_KV_HEADS, nb, Q_BLOCK), F32),
                        pltpu.VMEM((LANE, Q_BLOCK), F32),
                        pltpu.VMEM((n_state, 1, nq), F32),
                        pltpu.VMEM((n_state, HEAD_DIM + _ONES_ROWS, nq), F32),
                        pltpu.VMEM((A_KV_HEADS, KEY_CHUNK, nq), F32),
                        pltpu.VMEM((A_KV_HEADS, 1, nq), F32),
                        pltpu.VMEM((KEY_CHUNK, nq), BF16),
                        pltpu.VMEM((1, nq), F32)],
        compiler_params=_params(2),
        name="nsa_attention",
    )(q, kc, vct, ks3, vst, kw3, vwt, gates, block_onehot.reshape(1, seq, LANE))


def _rglru_kernel(x_ref, g_ref, win_ref, bin_ref, cw_ref, cb_ref, wai_ref, ba_ref, bi_ref, lam_ref,
                  wout_ref, bout_ref, o_ref, zbuf, abuf, bbuf, hbuf, hprev, *, tiles_per_batch):
    tm = x_ref.shape[0]
    i = pl.program_id(0)

    @pl.when(i % tiles_per_batch == 0)
    def _():
        zbuf[0:SUBLANE, :] = jnp.zeros((SUBLANE, R_WIDTH), F32)
        hprev[...] = jnp.zeros((SUBLANE, R_WIDTH), F32)

    x = x_ref[...]
    hb = _rms(x, g_ref[...]).astype(BF16)
    y = _gelu(_dot(hb, win_ref[:, 0:R_WIDTH]) + bin_ref[:, 0:R_WIDTH])
    z2 = _dot(hb, win_ref[:, R_WIDTH:2 * R_WIDTH]) + bin_ref[:, R_WIDTH:2 * R_WIDTH]
    zbuf[SUBLANE:SUBLANE + tm, :] = z2
    cw = cw_ref[...]
    u = (cw[0:1, :] * zbuf[pl.ds(SUBLANE - 3, tm), :]
         + cw[1:2, :] * zbuf[pl.ds(SUBLANE - 2, tm), :]
         + cw[2:3, :] * zbuf[pl.ds(SUBLANE - 1, tm), :]
         + cw[3:4, :] * z2 + cb_ref[...])
    zbuf[0:SUBLANE, :] = z2[tm - SUBLANE:tm, :]

    ub = u.astype(BF16)
    log_sig = jax.nn.log_sigmoid(lam_ref[...])
    for n in range(R_BLOCKS):
        cols = slice(n * R_BLOCK_DIM, (n + 1) * R_BLOCK_DIM)
        ri = _dot(ub[:, cols], wai_ref[n])
        r = _sigmoid(ri[:, 0:R_BLOCK_DIM] + ba_ref[:, cols])
        gi = _sigmoid(ri[:, R_BLOCK_DIM:2 * R_BLOCK_DIM] + bi_ref[:, cols])
        log_a = (LRU_C * log_sig[:, cols]) * r
        a = jnp.exp(log_a)
        mult = jnp.sqrt(-jnp.tanh(log_a) * (a * a + 1.0))
        abuf[:, cols] = a
        bbuf[:, cols] = mult * (gi * u[:, cols])

    rows = lax.broadcasted_iota(jnp.int32, (SUBLANE, R_WIDTH), 0)

    def slab(j, hp):
        r0 = pl.multiple_of(j * SUBLANE, SUBLANE)
        a = abuf[pl.ds(r0, SUBLANE), :]
        b = bbuf[pl.ds(r0, SUBLANE), :]
        for d in (1, 2, 4):
            keep = rows >= d
            b = jnp.where(keep, a * pltpu.roll(b, d, 0) + b, b)
            a = jnp.where(keep, a * pltpu.roll(a, d, 0), a)
        h = a * hp + b
        hbuf[pl.ds(r0, SUBLANE), :] = h
        return jnp.broadcast_to(h[SUBLANE - 1:SUBLANE, :], (SUBLANE, R_WIDTH))

    hprev[...] = lax.fori_loop(0, tm // SUBLANE, slab, hprev[...], unroll=4)
    o_ref[...] = x + _dot((hbuf[...] * y).astype(BF16), wout_ref[...]) + bout_ref[...]


def _rglru(x2, g, win, b_in, conv_w, conv_b, wai, b_a, b_i, lam, wout, b_out, seq):
    n = x2.shape[0]
    tm = ROW_TILE
    tpb = seq // tm
    row = pl.BlockSpec((tm, D_MODEL), lambda i: (i, 0))
    buf = lambda r: pltpu.VMEM((r, R_WIDTH), F32)
    return pl.pallas_call(
        functools.partial(_rglru_kernel, tiles_per_batch=tpb),
        out_shape=jax.ShapeDtypeStruct((n, D_MODEL), F32),
        grid=(n // tm,),
        in_specs=[row, _resident((1, D_MODEL)), _resident((D_MODEL, 2 * R_WIDTH)), _resident((1, 2 * R_WIDTH)),
                  _resident((R_CONV, R_WIDTH)), _resident((1, R_WIDTH)),
                  _resident((R_BLOCKS, R_BLOCK_DIM, 2 * R_BLOCK_DIM)),
                  _resident((1, R_WIDTH)), _resident((1, R_WIDTH)), _resident((1, R_WIDTH)),
                  _resident((R_WIDTH, D_MODEL)), _resident((1, D_MODEL))],
        out_specs=row,
        scratch_shapes=[buf(tm + SUBLANE), buf(tm), buf(tm), buf(tm), buf(SUBLANE)],
        compiler_params=_params(1),
        name="rglru_mixer",
    )(x2, g, win, b_in, conv_w, conv_b, wai, b_a, b_i, lam, wout, b_out)


def _mix_ffn_kernel(x_ref, ot_ref, ob_ref, wo_ref, *rest, **static):
    mixed = jnp.concatenate([ot_ref[0].T, ob_ref[...]], axis=1)
    _ffn_body(x_ref[...] + _dot(mixed, wo_ref[...]), *rest, **static)


def _ffn_kernel(x_ref, *rest, **static):
    _ffn_body(x_ref[...], *rest, **static)


def _ffn_body(x, g_ref, wup_ref, cw_ref, cb_ref, wdn_ref, gf_ref, o_ref, ubuf, carry, acc,
              *, tiles_per_batch, final_norm):
    tm = x.shape[0]
    i = pl.program_id(0)

    @pl.when(i % tiles_per_batch == 0)
    def _():
        carry[...] = jnp.zeros(carry.shape, F32)

    hb = _rms(x, g_ref[...]).astype(BF16)
    acc[...] = x
    chunks = [(c0, min(FF_CHUNK, D_FF - c0)) for c0 in range(0, D_FF, FF_CHUNK)]

    def up(ci):
        c0, width = chunks[ci]
        return [_dot(hb, wup_ref[:, half * D_FF + c0:half * D_FF + c0 + width]) for half in range(2)]

    def down(act, c0, width):
        acc[...] += _dot(act, wdn_ref[c0:c0 + width, :])

    u_next = up(0)
    pending = None
    for ci, (c0, width) in enumerate(chunks):
        u_cur = u_next
        if ci + 1 < len(chunks):
            u_next = up(ci + 1)
        branch = []
        for half in range(2):
            col = half * D_FF + c0
            cols = slice(col, col + width)
            u = u_cur[half]
            ubuf[half, 0:SUBLANE, 0:width] = carry[:, cols]
            ubuf[half, SUBLANE:SUBLANE + tm, 0:width] = u
            carry[:, cols] = u[tm - SUBLANE:tm, :]
            w = cw_ref[:, cols]
            branch.append(w[0:1, :] * ubuf[half, pl.ds(SUBLANE - 2, tm), 0:width]
                          + w[1:2, :] * ubuf[half, pl.ds(SUBLANE - 1, tm), 0:width]
                          + w[2:3, :] * u + cb_ref[:, cols])
        act = (branch[0] * jax.nn.sigmoid(branch[0]) * branch[1]).astype(BF16)
        if pending is not None:
            down(*pending)
        pending = (act, c0, width)
    down(*pending)
    out = acc[...]
    if final_norm:
        out = _rms(out, gf_ref[...])
    o_ref[...] = out


def _ffn(x2, mixer, g, wup, conv_w, conv_b, wdn, g_final, seq, final_norm):
    n = x2.shape[0]
    tm = ROW_TILE
    tpb = seq // tm
    row = lambda w: pl.BlockSpec((tm, w), lambda i: (i, 0))
    ffn_specs = [_resident((1, D_MODEL)), _resident((D_MODEL, 2 * D_FF)), _resident((3, 2 * D_FF)),
                 _resident((1, 2 * D_FF)), _resident((D_FF, D_MODEL)), _resident((1, D_MODEL))]
    ffn_args = (g, wup, conv_w, conv_b, wdn, g_final)
    if mixer is None:
        body, specs, args = _ffn_kernel, [row(D_MODEL)], (x2,)
    else:
        body = _mix_ffn_kernel
        specs = [row(D_MODEL), pl.BlockSpec((1, A_WIDTH, tm), lambda i: (i // tpb, 0, i % tpb)),
                 row(B_WIDTH), _resident((A_WIDTH + B_WIDTH, D_MODEL))]
        args = (x2,) + tuple(mixer)
    return pl.pallas_call(
        functools.partial(body, tiles_per_batch=tpb, final_norm=final_norm),
        out_shape=jax.ShapeDtypeStruct((n, D_MODEL), F32),
        grid=(n // tm,),
        in_specs=specs + ffn_specs,
        out_specs=row(D_MODEL),
        scratch_shapes=[pltpu.VMEM((2, tm + SUBLANE, FF_CHUNK), F32),
                        pltpu.VMEM((SUBLANE, 2 * D_FF), F32),
                        pltpu.VMEM((tm, D_MODEL), F32)],
        compiler_params=_params(1),
        name="conv_ffn",
    )(*args, *ffn_args)


def _rope_tables(seq):
    inv = 1.0 / (ROPE_THETA ** (jnp.arange(0, HEAD_DIM, 2, dtype=F32) / HEAD_DIM))
    ang = jnp.arange(seq, dtype=F32)[:, None] * inv[None, :]
    ang = jnp.tile(ang, (1, LANE // (HEAD_DIM // 2)))
    sign = jnp.where(jnp.arange(LANE) < LANE // 2, -1.0, 1.0).astype(F32)
    return jnp.cos(ang), jnp.sin(ang) * sign[None, :]


def _even_in_columns():
    kv0 = A_WIDTH
    part = lambda p: kv0 + p * KV_COLS
    g0 = kv0 + 6 * KV_COLS
    bc0 = g0 + A_HEADS * N_BRANCH
    n_gate = A_HEADS * N_BRANCH
    src = np.concatenate([
        _Q_PERM,
        part(0) + _K_PERM, part(2) + _K_PERM, part(4) + _K_PERM,
        part(1) + _L, part(3) + _L, part(5) + _L,
        g0 + _G_PERM, np.zeros(LANE - n_gate, np.int64),
        np.arange(bc0, bc0 + 3 * B_WIDTH),
    ])
    scale = np.ones(src.shape, np.float32)
    scale[:A_WIDTH] = (HEAD_DIM ** -0.5) * LOG2E
    scale[_C_GATE + n_gate:_C_BG] = 0.0
    assert src.shape == (_C_END,)
    return src, scale


_EVEN_SRC, _EVEN_SCALE = _even_in_columns()


def _even_in_weights(w_in):
    return (w_in[:, _EVEN_SRC] * _EVEN_SCALE[None, :]).astype(BF16)


def _compress_weights(pe, w1, w2, lane_head, lane_dim):
    onehot = jnp.asarray(lane_head[:, None] == np.arange(A_KV_HEADS)[None, :], F32)
    w1g = w1.reshape(CMP_LEN, HEAD_DIM, CMP_HIDDEN)[:, lane_dim, :]
    w1x = (w1g[:, :, None, :] * onehot[None, :, :, None]).reshape(CMP_LEN * LANE, A_KV_HEADS * CMP_HIDDEN)
    half = (CMP_LEN // 2) * LANE
    w1x = jnp.stack([w1x[:half], w1x[half:]]).astype(BF16)
    w2x = (onehot.T[:, None, :] * w2[:, lane_dim][None, :, :]).reshape(A_KV_HEADS * CMP_HIDDEN, LANE)
    pel = pe[:, lane_dim].reshape(2, 1, half)
    pex = jnp.broadcast_to(pel, (2, SUBLANE, half)).astype(BF16)
    return w1x, w2x.astype(BF16), pex


def kernel(x, norm_mix, norm_ffn, norm_final, a_w_in, a_cmp_pe, a_cmp_w1, a_cmp_w2, a_conv_w, a_w_out,
           c_w_in, c_b_in, c_conv_w, c_conv_b, c_w_a, c_b_a, c_w_i, c_b_i, c_lambda, c_w_out, c_b_out,
           f_w_up, f_conv_w, f_conv_b, f_w_down):
    batch, seq, _ = x.shape
    assert seq % ROW_TILE == 0 and ROW_TILE % KEY_CHUNK == 0 and seq // SLC_BLOCK >= 1
    x2 = x.reshape(batch * seq, D_MODEL)
    cos, sin = _rope_tables(seq)
    row = lambda v: v.reshape(1, -1)
    for layer in range(DEPTH):
        j = layer // 2
        g_mix = row(norm_mix[layer])
        if layer % 2 == 0:
            wcat = _even_in_weights(a_w_in[j])
            q, kcs, ks, kw, vcs, vst, vwt, gates, o_b = _even_in_proj(
                x2, g_mix, wcat, cos, sin, a_conv_w[j], batch, seq)
            ncp = seq // CMP_STRIDE
            w1k, w2k, pek = _compress_weights(a_cmp_pe[j, 0], a_cmp_w1[j, 0], a_cmp_w2[j, 0], _IL_HEAD, _IL_DIM)
            w1v, w2v, pev = _compress_weights(a_cmp_pe[j, 1], a_cmp_w1[j, 1], a_cmp_w2[j, 1], _ST_HEAD, _ST_DIM)
            kc, vct = _compress(kcs.reshape(batch, ncp, CMP_STRIDE * LANE),
                                vcs.reshape(batch, ncp, CMP_STRIDE * LANE),
                                w1k, w2k, pek, w1v, w2v, pev)
            o_t = _attention(q, kc, vct, ks, vst, kw, vwt, gates, batch, seq)
            mixer = (o_t, o_b, a_w_out[j].astype(BF16))
        else:
            mixer = None
            wai = jnp.concatenate([c_w_a[j], c_w_i[j]], axis=-1).astype(BF16)
            x2 = _rglru(x2, g_mix, c_w_in[j].astype(BF16), row(c_b_in[j]), c_conv_w[j], row(c_conv_b[j]),
                        wai, row(c_b_a[j]), row(c_b_i[j]), row(c_lambda[j]),
                        c_w_out[j].astype(BF16), row(c_b_out[j]), seq)
        x2 = _ffn(x2, mixer, row(norm_ffn[layer]), f_w_up[layer].astype(BF16), f_conv_w[layer],
                  row(f_conv_b[layer]), f_w_down[layer].astype(BF16), row(norm_final), seq,
                  final_norm=(layer == DEPTH - 1))
    return x2.reshape(batch, seq, D_MODEL)
```

```python
import functools
import math

import numpy as np
import jax
import jax.numpy as jnp
from jax import lax
from jax.experimental import pallas as pl
from jax.experimental.pallas import tpu as pltpu

F32 = jnp.float32
BF16 = jnp.bfloat16

D_MODEL = 1024
DEPTH = 4
A_HEADS = 8
A_KV_HEADS = 2
A_GROUP = A_HEADS // A_KV_HEADS
HEAD_DIM = 64
CMP_LEN = 32
CMP_STRIDE = 16
CMP_HIDDEN = 128
SLC_BLOCK = 64
SLC_TOPK = 8
WINDOW = 512
Q_BLOCK = 128
N_BRANCH = 3
ROPE_THETA = 10000.0
A_WIDTH = A_HEADS * HEAD_DIM
KV_COLS = A_KV_HEADS * HEAD_DIM
B_WIDTH = D_MODEL // 2
R_WIDTH = D_MODEL
R_BLOCKS = 8
R_BLOCK_DIM = R_WIDTH // R_BLOCKS
R_CONV = 4
LRU_C = 8.0
D_FF = 2816
NORM_EPS = 1e-6

LANE = 128
SUBLANE = 8
V7X_VMEM_BYTES = 64 * 1024 * 1024
VMEM_LIMIT = (V7X_VMEM_BYTES * 7) // 8

PROJ_ROW_TILE = 1024
FFN_ROW_TILE = 512
KEY_CHUNK = 256
M_FLOOR = -1e30
LOG2E = 1.4426950408889634

_L = np.arange(LANE)
_IL_HEAD = (_L // 32) % 2
_IL_DIM = (_L % 32) + 32 * (_L // 64)
_ST_HEAD = _L // HEAD_DIM
_ST_DIM = _L % HEAD_DIM
_K_PERM = _IL_HEAD * HEAD_DIM + _IL_DIM
_Q_PERM = np.concatenate([(_IL_HEAD * A_GROUP + g) * HEAD_DIM + _IL_DIM for g in range(A_GROUP)])
_G_PERM = np.array([hk * A_GROUP * N_BRANCH + g * N_BRANCH + br
                    for br in range(N_BRANCH) for hk in range(A_KV_HEADS) for g in range(A_GROUP)])

_C_Q = 0
_C_KCS = 512
_C_KS = 640
_C_KW = 768
_C_VCS = 896
_C_VS = 1024
_C_VW = 1152
_C_GATE = 1280
_C_BG = 1408
_C_CG = 1920
_C_XG = 2432
_C_END = 2944


def _resident(shape):
    nd = len(shape)
    return pl.BlockSpec(shape, lambda *_: (0,) * nd, pipeline_mode=pl.Buffered(1))


def _params(n_axes):
    return pltpu.CompilerParams(dimension_semantics=("arbitrary",) * n_axes,
                                vmem_limit_bytes=VMEM_LIMIT)


def _rms(x, g):
    return (x * lax.rsqrt(jnp.mean(x * x, axis=-1, keepdims=True) + NORM_EPS)) * g


def _gelu(x):
    return 0.5 * x * (1.0 + jnp.tanh(0.7978845608028654 * (x + 0.044715 * (x * x * x))))


def _sigmoid(x):
    return 0.5 * jnp.tanh(0.5 * x) + 0.5


def _dot(a, b):
    return jnp.dot(a, b, preferred_element_type=F32)


def _even_in_kernel(x_ref, g_ref, w_ref, cos_ref, sin_ref, cw_ref,
                    q_ref, kcs_ref, ks_ref, kw_ref, vcs_ref, vst_ref, vwt_ref, gate_ref, ob_ref,
                    cbuf, *, tiles_per_batch):
    tm = x_ref.shape[0]
    i = pl.program_id(0)
    hb = _rms(x_ref[...], g_ref[...]).astype(BF16)
    cos = cos_ref[...]
    sin = sin_ref[...]

    def rope(z):
        return z * cos + pltpu.roll(z, LANE // 2, 1) * sin

    zq = _dot(hb, w_ref[:, _C_Q:_C_KCS])
    for g in range(A_GROUP):
        q_ref[:, g * LANE:(g + 1) * LANE] = rope(zq[:, g * LANE:(g + 1) * LANE]).astype(BF16)

    zk = _dot(hb, w_ref[:, _C_KCS:_C_VS])
    kcs_ref[...] = rope(zk[:, 0:LANE]).astype(BF16)
    ks_ref[...] = rope(zk[:, LANE:2 * LANE]).astype(BF16)
    kw_ref[...] = rope(zk[:, 2 * LANE:3 * LANE]).astype(BF16)
    vcs_ref[...] = zk[:, 3 * LANE:4 * LANE].astype(BF16)

    zv = _dot(hb, w_ref[:, _C_VS:_C_BG])
    for c in range(tm // KEY_CHUNK):
        rows = slice(c * KEY_CHUNK, (c + 1) * KEY_CHUNK)
        vst_ref[0, c] = zv[rows, 0:LANE].T.astype(BF16)
        vwt_ref[0, c] = zv[rows, LANE:2 * LANE].T.astype(BF16)
    gate_ref[...] = jax.nn.sigmoid(zv[:, 2 * LANE:3 * LANE])

    @pl.when(i % tiles_per_batch == 0)
    def _():
        cbuf[0:SUBLANE, :] = jnp.zeros((SUBLANE, B_WIDTH), F32)

    c0 = _dot(hb, w_ref[:, _C_CG:_C_XG]) * _dot(hb, w_ref[:, _C_XG:_C_END])
    cbuf[SUBLANE:SUBLANE + tm, :] = c0
    cw = cw_ref[...]
    y = (cw[0:1, :] * cbuf[pl.ds(SUBLANE - 2, tm), :]
         + cw[1:2, :] * cbuf[pl.ds(SUBLANE - 1, tm), :]
         + cw[2:3, :] * c0)
    cbuf[0:SUBLANE, :] = c0[tm - SUBLANE:tm, :]
    ob_ref[...] = (_dot(hb, w_ref[:, _C_BG:_C_CG]) * y).astype(BF16)


def _even_in_proj(x2, g, wcat, cos, sin, conv_w, batch, seq):
    n = x2.shape[0]
    tm = PROJ_ROW_TILE
    tpb = seq // tm
    nch = seq // KEY_CHUNK
    row = lambda w: pl.BlockSpec((tm, w), lambda i: (i, 0))
    vt_spec = pl.BlockSpec((1, tm // KEY_CHUNK, LANE, KEY_CHUNK), lambda i: (i // tpb, i % tpb, 0, 0))
    tab_spec = pl.BlockSpec((tm, LANE), lambda i: (i % tpb, 0))
    out_shape = (
        jax.ShapeDtypeStruct((n, A_WIDTH), BF16),
        jax.ShapeDtypeStruct((n, LANE), BF16),
        jax.ShapeDtypeStruct((n, LANE), BF16),
        jax.ShapeDtypeStruct((n, LANE), BF16),
        jax.ShapeDtypeStruct((n, LANE), BF16),
        jax.ShapeDtypeStruct((batch, nch, LANE, KEY_CHUNK), BF16),
        jax.ShapeDtypeStruct((batch, nch, LANE, KEY_CHUNK), BF16),
        jax.ShapeDtypeStruct((n, LANE), F32),
        jax.ShapeDtypeStruct((n, B_WIDTH), BF16),
    )
    return pl.pallas_call(
        functools.partial(_even_in_kernel, tiles_per_batch=tpb),
        out_shape=out_shape,
        grid=(n // tm,),
        in_specs=[row(D_MODEL), _resident((1, D_MODEL)), _resident((D_MODEL, _C_END)),
                  tab_spec, tab_spec, _resident((3, B_WIDTH))],
        out_specs=(row(A_WIDTH), row(LANE), row(LANE), row(LANE), row(LANE),
                   vt_spec, vt_spec, row(LANE), row(B_WIDTH)),
        scratch_shapes=[pltpu.VMEM((tm + SUBLANE, B_WIDTH), F32)],
        compiler_params=_params(1),
        name="even_in_proj",
    )(x2, g, wcat, cos, sin, conv_w)


def _compress_kernel(k2_ref, v2_ref, w1k_ref, w2k_ref, pek_ref, w1v_ref, w2v_ref, pev_ref,
                     kc_ref, vct_ref, sbuf):
    ncp = k2_ref.shape[1]

    def one(x2, w1_ref, w2_ref, pe_ref):
        top = _dot(x2, w1_ref[0])
        sbuf[0:ncp, :] = _dot(x2, w1_ref[1])
        sbuf[ncp:ncp + SUBLANE, :] = jnp.zeros((SUBLANE, sbuf.shape[1]), F32)
        bias = _dot(pe_ref[0], w1_ref[0]) + _dot(pe_ref[1], w1_ref[1])
        hid = top + sbuf[pl.ds(1, ncp), :] + bias[0:1, :]
        return _dot(_gelu(hid).astype(BF16), w2_ref[...])

    kc_ref[0] = one(k2_ref[0], w1k_ref, w2k_ref, pek_ref).astype(BF16)
    vct_ref[0] = one(v2_ref[0], w1v_ref, w2v_ref, pev_ref).T.astype(BF16)


def _compress(k2, v2, w1k, w2k, pek, w1v, w2v, pev):
    batch, ncp, width = k2.shape
    hid2 = A_KV_HEADS * CMP_HIDDEN
    src = pl.BlockSpec((1, ncp, width), lambda b: (b, 0, 0))
    return pl.pallas_call(
        _compress_kernel,
        out_shape=(jax.ShapeDtypeStruct((batch, ncp, LANE), BF16),
                   jax.ShapeDtypeStruct((batch, LANE, ncp), BF16)),
        grid=(batch,),
        in_specs=[src, src,
                  _resident((2, width, hid2)), _resident((hid2, LANE)), _resident((2, SUBLANE, width)),
                  _resident((2, width, hid2)), _resident((hid2, LANE)), _resident((2, SUBLANE, width))],
        out_specs=(pl.BlockSpec((1, ncp, LANE), lambda b: (b, 0, 0)),
                   pl.BlockSpec((1, LANE, ncp), lambda b: (b, 0, 0))),
        scratch_shapes=[pltpu.VMEM((ncp + SUBLANE, hid2), F32)],
        compiler_params=_params(1),
        name="compress",
    )(k2, v2, w1k, w2k, pek, w1v, w2v, pev)


_SLC, _WIN = 0, 1
_ONES_ROWS = 16


def _attn_kernel(q_ref, kc_ref, vct_ref, ks_ref, vst_ref, kw_ref, vwt_ref, gate_ref, blk_ref, o_ref,
                 ps_ref, sel_ref, rank_ref, gt_ref, m_ref, acc_ref, snx_ref, pdf_ref, adf_ref,
                 *, n_sel):
    qb = pl.program_id(1)
    ncp = kc_ref.shape[1]
    per = SLC_BLOCK // CMP_STRIDE
    nb = ncp // per
    nq = A_GROUP * Q_BLOCK
    blocks_per_chunk = KEY_CHUNK // SLC_BLOCK
    t0 = qb * Q_BLOCK
    last = qb // (KEY_CHUNK // Q_BLOCK)
    n_pairs = last // 2
    s0, s1 = _SLC * A_KV_HEADS, _SLC * A_KV_HEADS + 1

    lane = lax.broadcasted_iota(jnp.int32, (Q_BLOCK, LANE), 1)
    qmats = []
    for hk in range(A_KV_HEADS):
        head_mask = jnp.where((lane // 32) % 2 == hk, 1.0, 0.0).astype(BF16)
        qmats.append(jnp.concatenate(
            [q_ref[:, g * LANE:(g + 1) * LANE] * head_mask for g in range(A_GROUP)], axis=0))
    q_t = [qm.T for qm in qmats]
    pos = t0 + lax.broadcasted_iota(jnp.int32, (1, nq), 1) % Q_BLOCK
    chunk_rows = lax.broadcasted_iota(jnp.int32, (KEY_CHUNK, 1), 0)
    ones_rows = jnp.ones((_ONES_ROWS, KEY_CHUNK), BF16)

    def key_chunk(k_ref, c):
        return k_ref[0, pl.ds(pl.multiple_of(c * KEY_CHUNK, KEY_CHUNK), KEY_CHUNK), :]

    def vt_of(vt_ref, c, hk):
        return jnp.concatenate([vt_ref[0, c, hk * HEAD_DIM:(hk + 1) * HEAD_DIM, :], ones_rows], axis=0)

    def softmax_update(slot, s):
        m_old = m_ref[slot]
        m_new = jnp.maximum(m_old, jnp.max(s, axis=0, keepdims=True))
        m_ref[slot] = m_new
        return jnp.exp2(s - m_new).astype(BF16), jnp.exp2(m_old - m_new)

    def pv_update(slot, vt, p, alpha):
        acc_ref[slot] = alpha * acc_ref[slot] + _dot(vt, p)

    def causal(s, c):
        return jnp.where(chunk_rows <= pos - c * KEY_CHUNK, s, -jnp.inf)

    for hk in range(A_KV_HEADS):
        snx_ref[hk] = _dot(key_chunk(ks_ref, 0), q_t[hk])
    m_ref[...] = jnp.full(m_ref.shape, M_FLOOR, F32)
    acc_ref[...] = jnp.zeros(acc_ref.shape, F32)
    pdf_ref[...] = jnp.zeros(pdf_ref.shape, BF16)
    adf_ref[...] = jnp.ones(adf_ref.shape, F32)

    cend = lax.broadcasted_iota(jnp.int32, (ncp, 1), 0) * CMP_STRIDE + (CMP_LEN - 1)
    bidx = lax.broadcasted_iota(jnp.int32, (nb, Q_BLOCK), 0)
    cur = (t0 + lax.broadcasted_iota(jnp.int32, (1, Q_BLOCK), 1)) // SLC_BLOCK
    forced = (bidx == 0) | (bidx == cur) | (bidx == cur - 1)
    o_cmp = []
    sc_all = [jnp.where(cend <= pos, _dot(kc_ref[0], q_t[hk]), -jnp.inf)
              for hk in range(A_KV_HEADS)]
    for hk in range(A_KV_HEADS):
        sc = sc_all[hk]
        m_c = jnp.maximum(jnp.max(sc, axis=0, keepdims=True), M_FLOOR)
        p_c = jnp.exp2(sc - m_c)
        pn = p_c * (1.0 / jnp.maximum(jnp.sum(p_c, axis=0, keepdims=True), 1e-30))
        o_cmp.append(_dot(vct_ref[0, hk * HEAD_DIM:(hk + 1) * HEAD_DIM, :], pn.astype(BF16)))
        psum = pn[:, 0:Q_BLOCK]
        for g in range(1, A_GROUP):
            psum = psum + pn[:, g * Q_BLOCK:(g + 1) * Q_BLOCK]
        ps_ref[hk, 0:SUBLANE, :] = jnp.zeros((SUBLANE, Q_BLOCK), F32)
        ps_ref[hk, SUBLANE:SUBLANE + ncp, :] = psum
        imp = ps_ref[hk, pl.ds(SUBLANE - 1, nb, stride=per), :]
        for k in range(per):
            imp = imp + ps_ref[hk, pl.ds(SUBLANE + k, nb, stride=per), :]
        sel_ref[hk, 0:nb, :] = jnp.where(forced, 1e4, jnp.where(bidx <= cur, imp, -1e4))

    rank_ref[...] = jnp.zeros(rank_ref.shape, F32)
    rows8 = lax.broadcasted_iota(jnp.int32, (SUBLANE, Q_BLOCK), 0)
    n_slab = nb // SUBLANE
    for grp in range(n_slab):
        @pl.when(grp * SUBLANE <= t0 // SLC_BLOCK + 1)
        def _():
            for hk in range(A_KV_HEADS):
                slabs = [sel_ref[hk, r * SUBLANE:(r + 1) * SUBLANE, :] for r in range(n_slab)]
                ranks = [rank_ref[hk, r * SUBLANE:(r + 1) * SUBLANE, :] for r in range(n_slab)]
                for k in range(grp * SUBLANE, (grp + 1) * SUBLANE):
                    rk = sel_ref[hk, k:k + 1, :]
                    for r in range(n_slab):
                        if r * SUBLANE > k:
                            beats = rk >= slabs[r]
                        elif (r + 1) * SUBLANE <= k:
                            beats = rk > slabs[r]
                        else:
                            beats = (rk > slabs[r]) | ((rk == slabs[r]) & (rows8 + r * SUBLANE > k))
                        ranks[r] = ranks[r] + jnp.where(beats, 1.0, 0.0)
                for r in range(n_slab):
                    rank_ref[hk, r * SUBLANE:(r + 1) * SUBLANE, :] = ranks[r]

    q_aug = []
    for hk in range(A_KV_HEADS):
        bias = jnp.where(rank_ref[hk] < n_sel, 0.0, M_FLOOR)
        sel_ref[hk, 0:nb, :] = bias
        if nb < LANE:
            sel_ref[hk, nb:LANE, :] = jnp.zeros((LANE - nb, Q_BLOCK), F32)
        bias_rows = jnp.concatenate([sel_ref[hk].astype(BF16)] * A_GROUP, axis=1)
        q_aug.append(jnp.concatenate([q_t[hk], bias_rows], axis=0))

    def slc_scores(c, hk):
        k_aug = jnp.concatenate([key_chunk(ks_ref, c), key_chunk(blk_ref, c)], axis=1)
        return _dot(k_aug, q_aug[hk])

    for hk in range(A_KV_HEADS):
        for bl in range(blocks_per_chunk):
            rows = slice(bl * SLC_BLOCK, (bl + 1) * SLC_BLOCK)
            row = jnp.concatenate([sel_ref[hk, bl:bl + 1, :]] * A_GROUP, axis=1)
            snx_ref[hk, rows, :] = snx_ref[hk, rows, :] + row

    def pair_body(i, carry):
        ca = 2 * i
        cb = ca + 1
        pv_update(s1, vt_of(vst_ref, jnp.maximum(ca - 1, 0), 1), pdf_ref[...], adf_ref[...])
        sc_ = slc_scores(cb, 0)
        sd = slc_scores(cb, 1)
        pa, aa = softmax_update(s0, snx_ref[0])
        pv_update(s0, vt_of(vst_ref, ca, 0), pa, aa)
        pb, ab = softmax_update(s1, snx_ref[1])
        snx_ref[0] = slc_scores(cb + 1, 0)
        pv_update(s1, vt_of(vst_ref, ca, 1), pb, ab)
        snx_ref[1] = slc_scores(cb + 1, 1)
        pc, ac = softmax_update(s0, sc_)
        pv_update(s0, vt_of(vst_ref, cb, 0), pc, ac)
        pd, ad = softmax_update(s1, sd)
        pdf_ref[...] = pd
        adf_ref[...] = ad
        return carry

    lax.fori_loop(0, n_pairs, pair_body, 0)

    c0 = 2 * n_pairs
    pv_update(s1, vt_of(vst_ref, jnp.maximum(c0 - 1, 0), 1), pdf_ref[...], adf_ref[...])

    def slc_tail(c, hk, prefetched):
        def score():
            return causal(snx_ref[hk] if prefetched else slc_scores(c, hk), c)
        return (_SLC * A_KV_HEADS + hk, score, lambda: vt_of(vst_ref, c, hk))

    def win_tail(c, hk, diag):
        cc = jnp.maximum(c, 0)

        def score():
            s = _dot(key_chunk(kw_ref, cc), q_t[hk])
            if diag:
                return causal(s, c)
            low = jnp.where(c >= 0, pos - WINDOW - c * KEY_CHUNK, KEY_CHUNK)
            return jnp.where(chunk_rows > low, s, -jnp.inf)
        return (_WIN * A_KV_HEADS + hk, score, lambda: vt_of(vwt_ref, cc, hk))

    units = [slc_tail(c0, 0, True), slc_tail(c0, 1, True),
             win_tail(last - 2, 0, False), win_tail(last - 2, 1, False),
             slc_tail(c0 + 1, 0, False), slc_tail(c0 + 1, 1, False),
             win_tail(last - 1, 0, False), win_tail(last - 1, 1, False),
             win_tail(last, 0, True), win_tail(last, 1, True)]
    ahead = 2
    scores = [u[1]() for u in units[:ahead]]
    for i, (slot, _, vt_fn) in enumerate(units):
        if i + ahead < len(units):
            scores.append(units[i + ahead][1]())
        p, alpha = softmax_update(slot, scores[i])
        pv_update(slot, vt_fn(), p, alpha)

    gt_ref[...] = gate_ref[...].T
    for hk in range(A_KV_HEADS):
        def gate(br):
            base = br * A_HEADS + hk * A_GROUP
            return jnp.concatenate([gt_ref[base + g:base + g + 1, :] for g in range(A_GROUP)], axis=1)

        def branch(slot):
            acc = acc_ref[slot]
            return acc[0:HEAD_DIM, :] * (1.0 / jnp.maximum(acc[HEAD_DIM:HEAD_DIM + 1, :], 1e-30))

        o = (gate(0) * o_cmp[hk] + gate(1) * branch(_SLC * A_KV_HEADS + hk)
             + gate(2) * branch(_WIN * A_KV_HEADS + hk))
        for g in range(A_GROUP):
            r0 = (hk * A_GROUP + g) * HEAD_DIM
            o_ref[0, r0:r0 + HEAD_DIM, :] = o[:, g * Q_BLOCK:(g + 1) * Q_BLOCK].astype(BF16)


def _attention(q, kc, vct, ks, vst, kw, vwt, gates, batch, seq):
    nqb = seq // Q_BLOCK
    ncp = kc.shape[1]
    nch = seq // KEY_CHUNK
    nb = seq // SLC_BLOCK
    assert nb <= LANE
    nq = A_GROUP * Q_BLOCK
    n_state = 2 * A_KV_HEADS
    ks3 = ks.reshape(batch, seq, LANE)
    kw3 = kw.reshape(batch, seq, LANE)
    block_onehot = (jnp.arange(seq)[:, None] // SLC_BLOCK == jnp.arange(LANE)[None, :]).astype(BF16)
    kspec = pl.BlockSpec((1, seq, LANE), lambda b, t: (b, 0, 0))
    vspec = pl.BlockSpec((1, nch, LANE, KEY_CHUNK), lambda b, t: (b, 0, 0, 0))
    return pl.pallas_call(
        functools.partial(_attn_kernel, n_sel=min(SLC_TOPK, nb)),
        out_shape=jax.ShapeDtypeStruct((batch, A_WIDTH, seq), BF16),
        grid=(batch, nqb),
        in_specs=[pl.BlockSpec((Q_BLOCK, A_WIDTH), lambda b, t: (b * nqb + t, 0)),
                  pl.BlockSpec((1, ncp, LANE), lambda b, t: (b, 0, 0)),
                  pl.BlockSpec((1, LANE, ncp), lambda b, t: (b, 0, 0)),
                  kspec, vspec, kspec, vspec,
                  pl.BlockSpec((Q_BLOCK, LANE), lambda b, t: (b * nqb + t, 0)),
                  _resident((1, seq, LANE))],
        out_specs=pl.BlockSpec((1, A_WIDTH, Q_BLOCK), lambda b, t: (b, 0, t)),
        scratch_shapes=[pltpu.VMEM((A_KV_HEADS, ncp + SUBLANE, Q_BLOCK), F32),
                        pltpu.VMEM((A_KV_HEADS, LANE, Q_BLOCK), F32),
                        pltpu.VMEM((A_KV_HEADS, nb, Q_BLOCK), F32),
                        pltpu.VMEM((LANE, Q_BLOCK), F32),
                        pltpu.VMEM((n_state, 1, nq), F32),
                        pltpu.VMEM((n_state, HEAD_DIM + _ONES_ROWS, nq), F32),
                        pltpu.VMEM((A_KV_HEADS, KEY_CHUNK, nq), F32),
                        pltpu.VMEM((KEY_CHUNK, nq), BF16),
                        pltpu.VMEM((1, nq), F32)],
        compiler_params=_params(2),
        name="nsa_attention",
    )(q, kc, vct, ks3, vst, kw3, vwt, gates, block_onehot.reshape(1, seq, LANE))


def _rglru_kernel(x_ref, g_ref, win_ref, bin_ref, cw_ref, cb_ref, wai_ref, ba_ref, bi_ref, lam_ref,
                  wout_ref, bout_ref, o_ref, zbuf, abuf, bbuf, hbuf, hprev, *, tiles_per_batch):
    tm = x_ref.shape[0]
    i = pl.program_id(0)

    @pl.when(i % tiles_per_batch == 0)
    def _():
        zbuf[0:SUBLANE, :] = jnp.zeros((SUBLANE, R_WIDTH), F32)
        hprev[...] = jnp.zeros((SUBLANE, R_WIDTH), F32)

    x = x_ref[...]
    hb = _rms(x, g_ref[...]).astype(BF16)
    y = _gelu(_dot(hb, win_ref[:, 0:R_WIDTH]) + bin_ref[:, 0:R_WIDTH])
    z2 = _dot(hb, win_ref[:, R_WIDTH:2 * R_WIDTH]) + bin_ref[:, R_WIDTH:2 * R_WIDTH]
    zbuf[SUBLANE:SUBLANE + tm, :] = z2
    cw = cw_ref[...]
    u = (cw[0:1, :] * zbuf[pl.ds(SUBLANE - 3, tm), :]
         + cw[1:2, :] * zbuf[pl.ds(SUBLANE - 2, tm), :]
         + cw[2:3, :] * zbuf[pl.ds(SUBLANE - 1, tm), :]
         + cw[3:4, :] * z2 + cb_ref[...])
    zbuf[0:SUBLANE, :] = z2[tm - SUBLANE:tm, :]

    ub = u.astype(BF16)
    log_sig = jax.nn.log_sigmoid(lam_ref[...])
    for n in range(R_BLOCKS):
        cols = slice(n * R_BLOCK_DIM, (n + 1) * R_BLOCK_DIM)
        ri = _dot(ub[:, cols], wai_ref[n])
        r = _sigmoid(ri[:, 0:R_BLOCK_DIM] + ba_ref[:, cols])
        gi = _sigmoid(ri[:, R_BLOCK_DIM:2 * R_BLOCK_DIM] + bi_ref[:, cols])
        log_a = (LRU_C * log_sig[:, cols]) * r
        a = jnp.exp(log_a)
        mult = jnp.sqrt(-jnp.tanh(log_a) * (a * a + 1.0))
        abuf[:, cols] = a
        bbuf[:, cols] = mult * (gi * u[:, cols])

    rows = lax.broadcasted_iota(jnp.int32, (SUBLANE, R_WIDTH), 0)

    def slab(j, hp):
        r0 = pl.multiple_of(j * SUBLANE, SUBLANE)
        a = abuf[pl.ds(r0, SUBLANE), :]
        b = bbuf[pl.ds(r0, SUBLANE), :]
        for d in (1, 2, 4):
            keep = rows >= d
            b = jnp.where(keep, a * pltpu.roll(b, d, 0) + b, b)
            a = jnp.where(keep, a * pltpu.roll(a, d, 0), a)
        h = a * hp + b
        hbuf[pl.ds(r0, SUBLANE), :] = h
        return jnp.broadcast_to(h[SUBLANE - 1:SUBLANE, :], (SUBLANE, R_WIDTH))

    hprev[...] = lax.fori_loop(0, tm // SUBLANE, slab, hprev[...], unroll=8)
    o_ref[...] = x + _dot((hbuf[...] * y).astype(BF16), wout_ref[...]) + bout_ref[...]


def _rglru(x2, g, win, b_in, conv_w, conv_b, wai, b_a, b_i, lam, wout, b_out, seq):
    n = x2.shape[0]
    tm = PROJ_ROW_TILE
    tpb = seq // tm
    row = pl.BlockSpec((tm, D_MODEL), lambda i: (i, 0))
    buf = lambda r: pltpu.VMEM((r, R_WIDTH), F32)
    return pl.pallas_call(
        functools.partial(_rglru_kernel, tiles_per_batch=tpb),
        out_shape=jax.ShapeDtypeStruct((n, D_MODEL), F32),
        grid=(n // tm,),
        in_specs=[row, _resident((1, D_MODEL)), _resident((D_MODEL, 2 * R_WIDTH)), _resident((1, 2 * R_WIDTH)),
                  _resident((R_CONV, R_WIDTH)), _resident((1, R_WIDTH)),
                  _resident((R_BLOCKS, R_BLOCK_DIM, 2 * R_BLOCK_DIM)),
                  _resident((1, R_WIDTH)), _resident((1, R_WIDTH)), _resident((1, R_WIDTH)),
                  _resident((R_WIDTH, D_MODEL)), _resident((1, D_MODEL))],
        out_specs=row,
        scratch_shapes=[buf(tm + SUBLANE), buf(tm), buf(tm), buf(tm), buf(SUBLANE)],
        compiler_params=_params(1),
        name="rglru_mixer",
    )(x2, g, win, b_in, conv_w, conv_b, wai, b_a, b_i, lam, wout, b_out)


def _mix_ffn_kernel(x_ref, ot_ref, ob_ref, wo_ref, *rest, **static):
    mixed = jnp.concatenate([ot_ref[0].T, ob_ref[...]], axis=1)
    _ffn_body(x_ref[...] + _dot(mixed, wo_ref[...]), *rest, **static)


def _ffn_kernel(x_ref, *rest, **static):
    _ffn_body(x_ref[...], *rest, **static)


def _ffn_body(x, g_ref, wup_ref, cw_ref, cb_ref, wdn_ref, gf_ref, o_ref, ubuf, carry,
              *, tiles_per_batch, final_norm):
    tm = x.shape[0]
    i = pl.program_id(0)

    @pl.when(i % tiles_per_batch == 0)
    def _():
        carry[...] = jnp.zeros(carry.shape, F32)

    hb = _rms(x, g_ref[...]).astype(BF16)
    branch = []
    for half in range(2):
        cols = slice(half * D_FF, (half + 1) * D_FF)
        u = _dot(hb, wup_ref[:, cols])
        ubuf[half, 0:SUBLANE, :] = carry[:, cols]
        ubuf[half, SUBLANE:SUBLANE + tm, :] = u
        carry[:, cols] = u[tm - SUBLANE:tm, :]
        w = cw_ref[:, cols]
        branch.append(w[0:1, :] * ubuf[half, pl.ds(SUBLANE - 2, tm), :]
                      + w[1:2, :] * ubuf[half, pl.ds(SUBLANE - 1, tm), :]
                      + w[2:3, :] * u + cb_ref[:, cols])
    act = (branch[0] * _sigmoid(branch[0]) * branch[1]).astype(BF16)
    out = x + _dot(act, wdn_ref[...])
    if final_norm:
        out = _rms(out, gf_ref[...])
    o_ref[...] = out


def _ffn(x2, mixer, g, wup, conv_w, conv_b, wdn, g_final, seq, final_norm):
    n = x2.shape[0]
    tm = FFN_ROW_TILE
    tpb = seq // tm
    row = lambda w: pl.BlockSpec((tm, w), lambda i: (i, 0))
    ffn_specs = [_resident((1, D_MODEL)), _resident((D_MODEL, 2 * D_FF)), _resident((3, 2 * D_FF)),
                 _resident((1, 2 * D_FF)), _resident((D_FF, D_MODEL)), _resident((1, D_MODEL))]
    ffn_args = (g, wup, conv_w, conv_b, wdn, g_final)
    if mixer is None:
        body, specs, args = _ffn_kernel, [row(D_MODEL)], (x2,)
    else:
        body = _mix_ffn_kernel
        specs = [row(D_MODEL), pl.BlockSpec((1, A_WIDTH, tm), lambda i: (i // tpb, 0, i % tpb)),
                 row(B_WIDTH), _resident((A_WIDTH + B_WIDTH, D_MODEL))]
        args = (x2,) + tuple(mixer)
    return pl.pallas_call(
        functools.partial(body, tiles_per_batch=tpb, final_norm=final_norm),
        out_shape=jax.ShapeDtypeStruct((n, D_MODEL), F32),
        grid=(n // tm,),
        in_specs=specs + ffn_specs,
        out_specs=row(D_MODEL),
        scratch_shapes=[pltpu.VMEM((2, tm + SUBLANE, D_FF), F32),
                        pltpu.VMEM((SUBLANE, 2 * D_FF), F32)],
        compiler_params=_params(1),
        name="conv_ffn",
    )(*args, *ffn_args)


def _rope_tables(seq):
    inv = 1.0 / (ROPE_THETA ** (jnp.arange(0, HEAD_DIM, 2, dtype=F32) / HEAD_DIM))
    ang = jnp.arange(seq, dtype=F32)[:, None] * inv[None, :]
    ang = jnp.tile(ang, (1, LANE // (HEAD_DIM // 2)))
    sign = jnp.where(jnp.arange(LANE) < LANE // 2, -1.0, 1.0).astype(F32)
    return jnp.cos(ang), jnp.sin(ang) * sign[None, :]


def _even_in_columns():
    kv0 = A_WIDTH
    part = lambda p: kv0 + p * KV_COLS
    g0 = kv0 + 6 * KV_COLS
    bc0 = g0 + A_HEADS * N_BRANCH
    n_gate = A_HEADS * N_BRANCH
    src = np.concatenate([
        _Q_PERM,
        part(0) + _K_PERM, part(2) + _K_PERM, part(4) + _K_PERM,
        part(1) + _L, part(3) + _L, part(5) + _L,
        g0 + _G_PERM, np.zeros(LANE - n_gate, np.int64),
        np.arange(bc0, bc0 + 3 * B_WIDTH),
    ])
    scale = np.ones(src.shape, np.float32)
    scale[:A_WIDTH] = (HEAD_DIM ** -0.5) * LOG2E
    scale[_C_GATE + n_gate:_C_BG] = 0.0
    assert src.shape == (_C_END,)
    return src, scale


_EVEN_SRC, _EVEN_SCALE = _even_in_columns()


def _even_in_weights(w_in):
    return (w_in[:, _EVEN_SRC] * _EVEN_SCALE[None, :]).astype(BF16)


def _compress_weights(pe, w1, w2, lane_head, lane_dim):
    onehot = jnp.asarray(lane_head[:, None] == np.arange(A_KV_HEADS)[None, :], F32)
    w1g = w1.reshape(CMP_LEN, HEAD_DIM, CMP_HIDDEN)[:, lane_dim, :]
    w1x = (w1g[:, :, None, :] * onehot[None, :, :, None]).reshape(CMP_LEN * LANE, A_KV_HEADS * CMP_HIDDEN)
    half = (CMP_LEN // 2) * LANE
    w1x = jnp.stack([w1x[:half], w1x[half:]]).astype(BF16)
    w2x = (onehot.T[:, None, :] * w2[:, lane_dim][None, :, :]).reshape(A_KV_HEADS * CMP_HIDDEN, LANE)
    pel = pe[:, lane_dim].reshape(2, 1, half)
    pex = jnp.broadcast_to(pel, (2, SUBLANE, half)).astype(BF16)
    return w1x, w2x.astype(BF16), pex


def kernel(x, norm_mix, norm_ffn, norm_final, a_w_in, a_cmp_pe, a_cmp_w1, a_cmp_w2, a_conv_w, a_w_out,
           c_w_in, c_b_in, c_conv_w, c_conv_b, c_w_a, c_b_a, c_w_i, c_b_i, c_lambda, c_w_out, c_b_out,
           f_w_up, f_conv_w, f_conv_b, f_w_down):
    batch, seq, _ = x.shape
    assert seq % PROJ_ROW_TILE == 0 and FFN_ROW_TILE % KEY_CHUNK == 0 and PROJ_ROW_TILE % FFN_ROW_TILE == 0
    x2 = x.reshape(batch * seq, D_MODEL)
    cos, sin = _rope_tables(seq)
    row = lambda v: v.reshape(1, -1)
    for layer in range(DEPTH):
        j = layer // 2
        g_mix = row(norm_mix[layer])
        if layer % 2 == 0:
            wcat = _even_in_weights(a_w_in[j])
            q, kcs, ks, kw, vcs, vst, vwt, gates, o_b = _even_in_proj(
                x2, g_mix, wcat, cos, sin, a_conv_w[j], batch, seq)
            ncp = seq // CMP_STRIDE
            w1k, w2k, pek = _compress_weights(a_cmp_pe[j, 0], a_cmp_w1[j, 0], a_cmp_w2[j, 0], _IL_HEAD, _IL_DIM)
            w1v, w2v, pev = _compress_weights(a_cmp_pe[j, 1], a_cmp_w1[j, 1], a_cmp_w2[j, 1], _ST_HEAD, _ST_DIM)
            kc, vct = _compress(kcs.reshape(batch, ncp, CMP_STRIDE * LANE),
                                vcs.reshape(batch, ncp, CMP_STRIDE * LANE),
                                w1k, w2k, pek, w1v, w2v, pev)
            o_t = _attention(q, kc, vct, ks, vst, kw, vwt, gates, batch, seq)
            mixer = (o_t, o_b, a_w_out[j].astype(BF16))
        else:
            mixer = None
            wai = jnp.concatenate([c_w_a[j], c_w_i[j]], axis=-1).astype(BF16)
            x2 = _rglru(x2, g_mix, c_w_in[j].astype(BF16), row(c_b_in[j]), c_conv_w[j], row(c_conv_b[j]),
                        wai, row(c_b_a[j]), row(c_b_i[j]), row(c_lambda[j]),
                        c_w_out[j].astype(BF16), row(c_b_out[j]), seq)
        x2 = _ffn(x2, mixer, row(norm_ffn[layer]), f_w_up[layer].astype(BF16), f_conv_w[layer],
                  row(f_conv_b[layer]), f_w_down[layer].astype(BF16), row(norm_final), seq,
                  final_norm=(layer == DEPTH - 1))
    return x2.reshape(batch, seq, D_MODEL)
```
